```python
import math
import jax, jax.numpy as jnp
from jax import lax
import numpy as np


D_MODEL = 2048
BATCH = 2
SEQ = 4096
DEPTH = 1

HEAD_DIM = 64
NSA_HEADS = 16
NSA_GROUPS = 4
NSA_HPG = NSA_HEADS // NSA_GROUPS
SB_HEADS = 16
CMP_LEN = 32
CMP_STRIDE = 16
CMP_HIDDEN = 128
SEL_LEN = 64
SEL_TOPK = 16
N_LOCAL_SEL = 2
SEL_FORCE_BONUS = 1e4
WINDOW = 512
Q_BLOCK = 128
REL_BUCKETS = 32
REL_MAX_DIST = 1024
PEER_HEADS = 8
PEER_KEYS = 128
PEER_EXPERTS = PEER_KEYS * PEER_KEYS
PEER_QDIM = 256
PEER_TOPK = 16
TOKEN_CHUNK = 128
EPS = 1e-6
NEG = -1e30

NSA_Q = NSA_HEADS * HEAD_DIM
NSA_KV = NSA_GROUPS * HEAD_DIM
SB_W = SB_HEADS * HEAD_DIM
IN_COLS = NSA_Q + 6 * NSA_KV + 3 * NSA_HEADS + 3 * SB_W + 2 * D_MODEL

kernel_name = 'hybrid_nsa_stickbreak_peer_block'


def _in_splits():
    sizes = [NSA_Q] + [NSA_KV] * 6 + [3 * NSA_HEADS, SB_W, SB_W, SB_W, D_MODEL, D_MODEL]
    out, acc = [], 0
    for s in sizes[:-1]:
        acc += s
        out.append(acc)
    return out


def rmsnorm(x, g):
    xf = x.astype(jnp.float32)
    y = xf * lax.rsqrt(jnp.mean(xf * xf, axis=-1, keepdims=True) + EPS)
    return (y * g.astype(jnp.float32)).astype(x.dtype)


def rel_bucket(dist):
    dist = jnp.maximum(dist, 0)
    n_exact = REL_BUCKETS // 2
    d_f = jnp.maximum(dist, 1).astype(jnp.float32)
    large = n_exact + (jnp.log(d_f / n_exact) / math.log(REL_MAX_DIST / n_exact)
                       * (REL_BUCKETS - n_exact)).astype(jnp.int32)
    large = jnp.minimum(large, REL_BUCKETS - 1)
    return jnp.where(dist < n_exact, dist, large)


def compress_tokens(tok, pe, w1, w2):
    B, T, G, d = tok.shape
    n_cmp = (T - CMP_LEN) // CMP_STRIDE + 1
    idx = jnp.arange(n_cmp)[:, None] * CMP_STRIDE + jnp.arange(CMP_LEN)[None, :]
    blk = tok[:, idx] + pe[None, None, :, None, :]
    blk = jnp.transpose(blk, (0, 1, 3, 2, 4)).reshape(B, n_cmp, G, CMP_LEN * d)
    return jax.nn.gelu(blk @ w1) @ w2


def nsa_compressed(q, kc, vc, rel_table):
    B, T, G, Hg, d = q.shape
    n_cmp = kc.shape[1]
    blk_end = jnp.arange(n_cmp)[None, :] * CMP_STRIDE + CMP_LEN - 1
    dist = jnp.arange(T)[:, None] - blk_end
    valid = dist >= 0
    bias = jnp.transpose(rel_table[rel_bucket(dist)], (2, 0, 1)).reshape(G, Hg, T, n_cmp)
    s = jnp.einsum('btghd,bngd->bghtn', q, kc).astype(jnp.float32) * (HEAD_DIM ** -0.5) + bias
    s = jnp.where(valid, s, NEG)
    p = jnp.where(valid, jax.nn.softmax(s, axis=-1), 0.0)
    o = jnp.einsum('bghtn,bngd->btghd', p.astype(vc.dtype), vc)
    return o, p


def nsa_select_indices(p_cmp, T):
    n_cmp = p_cmp.shape[-1]
    n_sel = T // SEL_LEN
    c_start = jnp.arange(n_cmp) * CMP_STRIDE
    s_start = jnp.arange(n_sel) * SEL_LEN
    overlap = jnp.maximum(jnp.minimum(c_start[:, None] + CMP_LEN, s_start[None, :] + SEL_LEN)
                          - jnp.maximum(c_start[:, None], s_start[None, :]), 0).astype(jnp.float32) / CMP_LEN
    imp = jnp.einsum('bghtn,nj->bgtj', p_cmp, overlap)
    cur = jnp.arange(T)[:, None] // SEL_LEN
    j = jnp.arange(n_sel)[None, :]
    valid = j <= cur
    forced = (j == 0) | ((cur - j >= 0) & (cur - j < N_LOCAL_SEL))
    score = jnp.where(valid, imp + SEL_FORCE_BONUS * forced.astype(jnp.float32), -jnp.inf)
    top_s, top_i = lax.top_k(score, min(SEL_TOPK, n_sel))
    return top_i, jnp.isfinite(top_s)


def nsa_selected(q, ks, vs, sel_idx, sel_ok, rel_table):
    B, T, G, Hg, d = q.shape
    n_sel = T // SEL_LEN
    k = sel_idx.shape[-1]
    nq = T // Q_BLOCK
    kb = ks.reshape(B, n_sel, SEL_LEN, G, d).transpose(0, 3, 1, 2, 4)
    vb = vs.reshape(B, n_sel, SEL_LEN, G, d).transpose(0, 3, 1, 2, 4)
    table_g = rel_table.reshape(REL_BUCKETS, G, Hg).transpose(1, 0, 2)
    qs = q.reshape(B, nq, Q_BLOCK, G, Hg, d).transpose(1, 0, 2, 3, 4, 5)
    idx_s = sel_idx.reshape(B, G, nq, Q_BLOCK, k).transpose(2, 0, 1, 3, 4)
    ok_s = sel_ok.reshape(B, G, nq, Q_BLOCK, k).transpose(2, 0, 1, 3, 4)
    b_ix = jnp.arange(B)[:, None, None, None]
    g_ix = jnp.arange(G)[None, :, None, None]

    def body(args):
        qi, ii, oki, blk = args
        kg = kb[b_ix, g_ix, ii].reshape(B, G, Q_BLOCK, k * SEL_LEN, d)
        vg = vb[b_ix, g_ix, ii].reshape(B, G, Q_BLOCK, k * SEL_LEN, d)
        pos = (ii[..., None] * SEL_LEN + jnp.arange(SEL_LEN)).reshape(B, G, Q_BLOCK, k * SEL_LEN)
        t = blk * Q_BLOCK + jnp.arange(Q_BLOCK)
        dist = t[None, None, :, None] - pos
        mask = (dist >= 0) & jnp.repeat(oki, SEL_LEN, axis=-1)
        bias = jax.vmap(lambda tab, bk: tab[bk], in_axes=(0, 1), out_axes=1)(table_g, rel_bucket(dist))
        s = (jnp.einsum('bqghd,bgqsd->bghqs', qi, kg).astype(jnp.float32) * (HEAD_DIM ** -0.5)
             + bias.transpose(0, 1, 4, 2, 3))
        m = mask[:, :, None]
        p = jnp.where(m, jax.nn.softmax(jnp.where(m, s, NEG), axis=-1), 0.0)
        return jnp.einsum('bghqs,bgqsd->bqghd', p.astype(vg.dtype), vg)

    out = lax.map(body, (qs, idx_s, ok_s, jnp.arange(nq)))
    return out.transpose(1, 0, 2, 3, 4, 5).reshape(B, T, G, Hg, d)


def nsa_window(q, kw, vw, rel_table):
    B, T, G, Hg, d = q.shape
    nq = T // Q_BLOCK
    S = WINDOW + Q_BLOCK
    kp = jnp.pad(kw, ((0, 0), (WINDOW, 0), (0, 0), (0, 0)))
    vp = jnp.pad(vw, ((0, 0), (WINDOW, 0), (0, 0), (0, 0)))
    qs = q.reshape(B, nq, Q_BLOCK, G, Hg, d).transpose(1, 0, 2, 3, 4, 5)
    rel = jnp.arange(Q_BLOCK)[:, None] + WINDOW - jnp.arange(S)[None, :]
    bias = jnp.transpose(rel_table[rel_bucket(rel)], (2, 0, 1)).reshape(G, Hg, Q_BLOCK, S)

    def body(args):
        qi, blk = args
        start = blk * Q_BLOCK
        kk = lax.dynamic_slice_in_dim(kp, start, S, axis=1)
        vv = lax.dynamic_slice_in_dim(vp, start, S, axis=1)
        key_pos = start - WINDOW + jnp.arange(S)
        mask = (rel >= 0) & (rel < WINDOW) & (key_pos[None, :] >= 0)
        s = jnp.einsum('bqghd,bsgd->bghqs', qi, kk).astype(jnp.float32) * (HEAD_DIM ** -0.5) + bias
        p = jax.nn.softmax(jnp.where(mask, s, NEG), axis=-1)
        return jnp.einsum('bghqs,bsgd->bqghd', p.astype(vv.dtype), vv)

    out = lax.map(body, (qs, jnp.arange(nq)))
    return out.transpose(1, 0, 2, 3, 4, 5).reshape(B, T, G, Hg, d)


def stick_breaking(q, k, v):
    B, T, H, d = q.shape
    nq = T // Q_BLOCK
    qs = q.reshape(B, nq, Q_BLOCK, H, d).transpose(1, 0, 2, 3, 4)
    key_pos = jnp.arange(T)

    def body(args):
        qi, blk = args
        t = blk * Q_BLOCK + jnp.arange(Q_BLOCK)
        mask = key_pos[None, :] < t[:, None]
        z = jnp.einsum('bqhd,bshd->bhqs', qi, k).astype(jnp.float32) * (HEAD_DIM ** -0.5)
        log_beta = jax.nn.log_sigmoid(z)
        log_rem = jnp.where(mask, log_beta - z, 0.0)
        suffix = lax.cumsum(log_rem, axis=3, reverse=True) - log_rem
        a = jnp.where(mask, jnp.exp(log_beta + suffix), 0.0)
        return jnp.einsum('bhqs,bshd->bqhd', a.astype(v.dtype), v)

    out = lax.map(body, (qs, jnp.arange(nq)))
    return out.transpose(1, 0, 2, 3, 4).reshape(B, T, H * d)


def peer_ffn(x, w_q, sub_keys, u, v):
    B, T, D = x.shape
    N = B * T
    xf = x.reshape(N, D)
    q = (xf @ w_q).reshape(N, PEER_HEADS, 2, PEER_QDIM // 2)
    s = jnp.einsum('nhpc,hpkc->nhpk', q, sub_keys).astype(jnp.float32)
    top_v, top_i = lax.top_k(s, PEER_TOPK)
    cand = top_v[:, :, 0, :, None] + top_v[:, :, 1, None, :]
    cand_i = top_i[:, :, 0, :, None] * PEER_KEYS + top_i[:, :, 1, None, :]
    best_s, best_j = lax.top_k(cand.reshape(N, PEER_HEADS, PEER_TOPK * PEER_TOPK), PEER_TOPK)
    expert = jnp.take_along_axis(cand_i.reshape(N, PEER_HEADS, PEER_TOPK * PEER_TOPK), best_j, axis=-1)
    gate = jax.nn.softmax(best_s, axis=-1)
    nc = N // TOKEN_CHUNK
    E = PEER_HEADS * PEER_TOPK
    xs = (xf.reshape(nc, TOKEN_CHUNK, D), expert.reshape(nc, TOKEN_CHUNK, E), gate.reshape(nc, TOKEN_CHUNK, E))

    def body(args):
        xc, ec, gc = args
        act = jax.nn.gelu(jnp.einsum('cd,ced->ce', xc, u[ec]))
        return jnp.einsum('ce,ced->cd', (gc * act).astype(v.dtype), v[ec])

    return lax.map(body, xs).reshape(B, T, D)


def hybrid_layer(h, attn_g, w_in, ck_pe, ck_w1, ck_w2, cv_pe, cv_w1, cv_w2, rel_table,
                 w_br_nsa, w_br_sb, w_out, ffn_g, pq, psk, pu, pv):
    B, T, _ = h.shape
    a = rmsnorm(h, attn_g)
    proj = a @ w_in
    (q_n, kc, vc, ks, vs, kw, vw, g_br, q_b, k_b, v_b, g_a, g_b) = jnp.split(proj, _in_splits(), axis=-1)
    G, Hg, d = NSA_GROUPS, NSA_HPG, HEAD_DIM
    q_n = q_n.reshape(B, T, G, Hg, d)
    kv = lambda z: z.reshape(B, T, G, d)
    kc_blk = compress_tokens(kv(kc), ck_pe, ck_w1, ck_w2)
    vc_blk = compress_tokens(kv(vc), cv_pe, cv_w1, cv_w2)
    o_c, p_c = nsa_compressed(q_n, kc_blk, vc_blk, rel_table)
    sel_idx, sel_ok = nsa_select_indices(p_c, T)
    o_s = nsa_selected(q_n, kv(ks), kv(vs), sel_idx, sel_ok, rel_table)
    o_w = nsa_window(q_n, kv(kw), kv(vw), rel_table)
    gb = jax.nn.sigmoid(g_br).reshape(B, T, 3, G, Hg, 1)
    o_nsa = (gb[:, :, 0] * o_c + gb[:, :, 1] * o_s + gb[:, :, 2] * o_w).reshape(B, T, NSA_Q)
    shp = (B, T, SB_HEADS, HEAD_DIM)
    o_sb = stick_breaking(q_b.reshape(shp), k_b.reshape(shp), v_b.reshape(shp))
    merged = jax.nn.sigmoid(g_a) * (o_nsa @ w_br_nsa) + jax.nn.sigmoid(g_b) * (o_sb @ w_br_sb)
    h = h + merged @ w_out
    h = h + peer_ffn(rmsnorm(h, ffn_g), pq, psk, pu, pv)
    return h


def setup_inputs(seed: int = 0) -> dict:
    key = jax.random.key(seed)
    ks = jax.random.split(key, 20)
    f32 = jnp.float32
    nrm = lambda k, shape, s: jax.random.normal(k, shape, f32) * s
    L = DEPTH
    return {
        'x': nrm(ks[0], (BATCH, SEQ, D_MODEL), 1.0),
        'attn_norm_g': 1.0 + nrm(ks[1], (L, D_MODEL), 0.01),
        'w_in': nrm(ks[2], (L, D_MODEL, IN_COLS), D_MODEL ** -0.5),
        'cmp_k_pe': nrm(ks[3], (L, CMP_LEN, HEAD_DIM), 0.1),
        'cmp_k_w1': nrm(ks[4], (L, CMP_LEN * HEAD_DIM, CMP_HIDDEN), (CMP_LEN * HEAD_DIM) ** -0.5),
        'cmp_k_w2': nrm(ks[5], (L, CMP_HIDDEN, HEAD_DIM), CMP_HIDDEN ** -0.5),
        'cmp_v_pe': nrm(ks[6], (L, CMP_LEN, HEAD_DIM), 0.1),
        'cmp_v_w1': nrm(ks[7], (L, CMP_LEN * HEAD_DIM, CMP_HIDDEN), (CMP_LEN * HEAD_DIM) ** -0.5),
        'cmp_v_w2': nrm(ks[8], (L, CMP_HIDDEN, HEAD_DIM), CMP_HIDDEN ** -0.5),
        'rel_bias_table': nrm(ks[9], (REL_BUCKETS, NSA_HEADS), 0.2),
        'w_branch_nsa': nrm(ks[10], (L, NSA_Q, D_MODEL), NSA_Q ** -0.5),
        'w_branch_sb': nrm(ks[11], (L, SB_W, D_MODEL), SB_W ** -0.5),
        'w_out': nrm(ks[12], (L, D_MODEL, D_MODEL), D_MODEL ** -0.5),
        'ffn_norm_g': 1.0 + nrm(ks[13], (L, D_MODEL), 0.01),
        'peer_w_q': nrm(ks[14], (L, D_MODEL, PEER_HEADS * PEER_QDIM), D_MODEL ** -0.5),
        'peer_sub_keys': nrm(ks[15], (L, PEER_HEADS, 2, PEER_KEYS, PEER_QDIM // 2), (PEER_QDIM // 2) ** -0.5),
        'peer_u': nrm(ks[16], (L, PEER_EXPERTS, D_MODEL), D_MODEL ** -0.5),
        'peer_v': nrm(ks[17], (L, PEER_EXPERTS, D_MODEL), PEER_HEADS ** -0.5),
        'final_norm_g': 1.0 + nrm(ks[18], (D_MODEL,), 0.01),
    }


def reference(x, attn_norm_g, w_in, cmp_k_pe, cmp_k_w1, cmp_k_w2, cmp_v_pe, cmp_v_w1, cmp_v_w2,
              rel_bias_table, w_branch_nsa, w_branch_sb, w_out, ffn_norm_g, peer_w_q, peer_sub_keys,
              peer_u, peer_v, final_norm_g):
    h = x
    for l in range(DEPTH):
        h = hybrid_layer(h, attn_norm_g[l], w_in[l], cmp_k_pe[l], cmp_k_w1[l], cmp_k_w2[l],
                         cmp_v_pe[l], cmp_v_w1[l], cmp_v_w2[l], rel_bias_table,
                         w_branch_nsa[l], w_branch_sb[l], w_out[l], ffn_norm_g[l],
                         peer_w_q[l], peer_sub_keys[l], peer_u[l], peer_v[l])
    return rmsnorm(h, final_norm_g)
```

```python
import functools
import math

import jax
import jax.numpy as jnp
import numpy as np
from jax import lax
from jax.experimental import pallas as pl
from jax.experimental.pallas import tpu as pltpu

F32 = jnp.float32
BF16 = jnp.bfloat16

HEAD_DIM = 64
NSA_HEADS = 16
NSA_GROUPS = 4
NSA_HPG = NSA_HEADS // NSA_GROUPS
SB_HEADS = 16
CMP_LEN = 32
CMP_STRIDE = 16
SEL_LEN = 64
SEL_TOPK = 16
N_LOCAL_SEL = 2
SEL_FORCE_BONUS = 1e4
WINDOW = 512
REL_BUCKETS = 32
REL_MAX_DIST = 1024
PEER_HEADS = 8
PEER_KEYS = 128
PEER_TOPK = 16
EPS = 1e-6
NEG = -1e30

NSA_Q = NSA_HEADS * HEAD_DIM
NSA_KV = NSA_GROUPS * HEAD_DIM
SB_W = SB_HEADS * HEAD_DIM

LANES = 128
TQ = 128
SB_SKIP_LOG = -110.0

_NT = (((1,), (1,)), ((), ()))


def _cparams(sem, vmem_mb=48):
    return pltpu.CompilerParams(dimension_semantics=sem, vmem_limit_bytes=vmem_mb * 1024 * 1024)


def _dot(a, b):
    return jnp.dot(a, b, preferred_element_type=F32)


def _dot_nt(a, b):
    return lax.dot_general(a, b, _NT, preferred_element_type=F32)


def _split_bf16(x):
    hi = x.astype(BF16)
    lo = (x - hi.astype(F32)).astype(BF16)
    return hi, lo


def _rmsnorm_body(x_ref, g_ref, o_ref):
    x = x_ref[...]
    y = x * lax.rsqrt(jnp.mean(x * x, axis=-1, keepdims=True) + EPS)
    o_ref[...] = (y * g_ref[...]).astype(o_ref.dtype)


def _rmsnorm(x, g, out_dtype, tm=512):
    n, d = x.shape
    return pl.pallas_call(
        _rmsnorm_body,
        grid=(n // tm,),
        in_specs=[pl.BlockSpec((tm, d), lambda i: (i, 0)), pl.BlockSpec((1, d), lambda i: (0, 0))],
        out_specs=pl.BlockSpec((tm, d), lambda i: (i, 0)),
        out_shape=jax.ShapeDtypeStruct((n, d), out_dtype),
        compiler_params=_cparams(("parallel",)),
        name="rmsnorm",
    )(x, g.reshape(1, d))


def _add_rmsnorm_body(x_ref, y_ref, g_ref, o_ref):
    x = x_ref[...] + y_ref[...]
    y = x * lax.rsqrt(jnp.mean(x * x, axis=-1, keepdims=True) + EPS)
    o_ref[...] = (y * g_ref[...]).astype(o_ref.dtype)


def _add_rmsnorm(x, y, g, tm=512):
    n, d = x.shape
    return pl.pallas_call(
        _add_rmsnorm_body,
        grid=(n // tm,),
        in_specs=[pl.BlockSpec((tm, d), lambda i: (i, 0)), pl.BlockSpec((tm, d), lambda i: (i, 0)),
                  pl.BlockSpec((1, d), lambda i: (0, 0))],
        out_specs=pl.BlockSpec((tm, d), lambda i: (i, 0)),
        out_shape=jax.ShapeDtypeStruct((n, d), F32),
        compiler_params=_cparams(("parallel",)),
        name="add_rmsnorm",
    )(x, y, g.reshape(1, d))


def _mm_body(a_ref, w_ref, o_ref):
    o_ref[...] = _dot(a_ref[...], w_ref[...]).astype(o_ref.dtype)


def _mm_res_body(a_ref, w_ref, r_ref, o_ref):
    o_ref[...] = (r_ref[...] + _dot(a_ref[...], w_ref[...])).astype(o_ref.dtype)


def _matmul(a, w, tm, tn, out_dtype, res=None, name="matmul"):
    m, k = a.shape
    nc = w.shape[1]
    in_specs = [pl.BlockSpec((tm, k), lambda j, i: (i, 0)), pl.BlockSpec((k, tn), lambda j, i: (0, j))]
    args = [a, w]
    body = _mm_body
    if res is not None:
        in_specs.append(pl.BlockSpec((tm, tn), lambda j, i: (i, j)))
        args.append(res)
        body = _mm_res_body
    return pl.pallas_call(
        body,
        grid=(nc // tn, m // tm),
        in_specs=in_specs,
        out_specs=pl.BlockSpec((tm, tn), lambda j, i: (i, j)),
        out_shape=jax.ShapeDtypeStruct((m, nc), out_dtype),
        compiler_params=_cparams(("parallel", "parallel")),
        name=name,
    )(*args)


def _rel_bucket(dist):
    dist = jnp.maximum(dist, 0)
    n_exact = REL_BUCKETS // 2
    d_f = jnp.maximum(dist, 1).astype(jnp.float32)
    large = n_exact + (jnp.log(d_f / n_exact) / math.log(REL_MAX_DIST / n_exact)
                       * (REL_BUCKETS - n_exact)).astype(jnp.int32)
    large = jnp.minimum(large, REL_BUCKETS - 1)
    return jnp.where(dist < n_exact, dist, large)


def _bias_expand_body(tab_ref, bkt_ref, o_ref):
    h = pl.program_id(1)
    bkt = bkt_ref[...]
    acc = jnp.full(bkt.shape, tab_ref[0, h], F32)
    for k in range(1, REL_BUCKETS):
        acc = jnp.where(bkt == k, tab_ref[k, h], acc)
    o_ref[0] = acc


def _bias_expand(table, bucket, tr):
    r, c = bucket.shape
    return pl.pallas_call(
        _bias_expand_body,
        grid=(r // tr, NSA_HEADS),
        in_specs=[pl.BlockSpec(memory_space=pltpu.SMEM), pl.BlockSpec((tr, c), lambda i, h: (i, 0))],
        out_specs=pl.BlockSpec((1, tr, c), lambda i, h: (h, i, 0)),
        out_shape=jax.ShapeDtypeStruct((NSA_HEADS, r, c), F32),
        compiler_params=_cparams(("parallel", "parallel")),
        name="bias_expand",
    )(table, bucket)


def _compress_body(blk_ref, pe_ref, w1_ref, w2_ref, o_ref):
    blk = (blk_ref[...].astype(F32) + pe_ref[...]).astype(BF16)
    h = jax.nn.gelu(_dot(blk, w1_ref[...]))
    o_ref[...] = _dot(h.astype(BF16), w2_ref[...]).astype(o_ref.dtype)


def _compress(blk, pe_flat, w1, w2, tr):
    r, kd = blk.shape
    hid = w1.shape[1]
    d = w2.shape[1]
    return pl.pallas_call(
        _compress_body,
        grid=(r // tr,),
        in_specs=[pl.BlockSpec((tr, kd), lambda i: (i, 0)), pl.BlockSpec((1, kd), lambda i: (0, 0)),
                  pl.BlockSpec((kd, hid), lambda i: (0, 0)), pl.BlockSpec((hid, d), lambda i: (0, 0))],
        out_specs=pl.BlockSpec((tr, d), lambda i: (i, 0)),
        out_shape=jax.ShapeDtypeStruct((r, d), BF16),
        compiler_params=_cparams(("parallel",)),
        name="compress_mlp",
    )(blk, pe_flat, w1.astype(BF16), w2.astype(BF16))


def _cmp_attn_body(q_ref, kc_ref, vc_ref, bias_ref, ov_ref, o_ref, sel_ref, *, n_cmp, n_top):
    i = pl.program_id(2)
    kc = kc_ref[0, 0]
    vc = vc_ref[0, 0]
    ncp = kc.shape[0]
    row = lax.broadcasted_iota(jnp.int32, (TQ, ncp), 0)
    col = lax.broadcasted_iota(jnp.int32, (TQ, ncp), 1)
    t = i * TQ + row
    valid = ((t - CMP_STRIDE * col - (CMP_LEN - 1)) >= 0) & (col < n_cmp)
    psum = jnp.zeros((TQ, ncp), F32)
    for h in range(NSA_HPG):
        s = _dot_nt(q_ref[0, 0, h], kc) + bias_ref[h]
        s = jnp.where(valid, s, NEG)
        m = jnp.max(s, axis=-1, keepdims=True)
        e = jnp.where(valid, jnp.exp(s - m), 0.0)
        l = jnp.sum(e, axis=-1, keepdims=True)
        p = e / jnp.maximum(l, 1e-30)
        o_ref[0, 0, h] = _dot(p.astype(BF16), vc)
        psum = psum + p
    hi, lo = _split_bf16(psum)
    ov = ov_ref[...]
    imp = _dot(hi, ov) + _dot(lo, ov)
    row2 = lax.broadcasted_iota(jnp.int32, (TQ, LANES), 0)
    col2 = lax.broadcasted_iota(jnp.int32, (TQ, LANES), 1)
    cur = jnp.right_shift(i * TQ + row2, int(math.log2(SEL_LEN)))
    gap = cur - col2
    forced = (col2 == 0) | ((gap >= 0) & (gap < N_LOCAL_SEL))
    score = jnp.where(col2 <= cur, imp + SEL_FORCE_BONUS * forced.astype(F32), -jnp.inf)
    colf = col2.astype(F32)
    sel = jnp.zeros((TQ, LANES), F32)
    for _ in range(n_top):
        m = jnp.max(score, axis=-1, keepdims=True)
        idx = jnp.min(jnp.where(score == m, colf, float(LANES)), axis=-1, keepdims=True)
        pick = colf == idx
        sel = jnp.where(pick & (m > -jnp.inf), 1.0, sel)
        score = jnp.where(pick, -jnp.inf, score)
    sel_ref[0, 0] = sel.astype(sel_ref.dtype)


def _cmp_attention(q, kc, vc, bias_c, overlap, n_cmp, n_top):
    b, g, hg, t, d = q.shape
    ncp = kc.shape[2]
    return pl.pallas_call(
        functools.partial(_cmp_attn_body, n_cmp=n_cmp, n_top=n_top),
        grid=(b, g, t // TQ),
        in_specs=[
            pl.BlockSpec((1, 1, hg, TQ, d), lambda bi, gi, i: (bi, gi, 0, i, 0)),
            pl.BlockSpec((1, 1, ncp, d), lambda bi, gi, i: (bi, gi, 0, 0)),
            pl.BlockSpec((1, 1, ncp, d), lambda bi, gi, i: (bi, gi, 0, 0)),
            pl.BlockSpec((hg, TQ, ncp), lambda bi, gi, i: (gi, i, 0)),
            pl.BlockSpec((ncp, LANES), lambda bi, gi, i: (0, 0)),
        ],
        out_specs=[
            pl.BlockSpec((1, 1, hg, TQ, d), lambda bi, gi, i: (bi, gi, 0, i, 0)),
            pl.BlockSpec((1, 1, TQ, LANES), lambda bi, gi, i: (bi, gi, i, 0)),
        ],
        out_shape=[jax.ShapeDtypeStruct((b, g, hg, t, d), F32), jax.ShapeDtypeStruct((b, g, t, LANES), BF16)],
        compiler_params=_cparams(("parallel", "parallel", "parallel")),
        name="cmp_attention",
    )(q, kc, vc, bias_c, overlap)


def _flash_body(q_ref, k_ref, v_ref, bias_ref, *rest, selected, n_bias):
    if selected:
        sel_ref, exp_ref, o_ref = rest
        selv = sel_ref[0, 0]
    else:
        (o_ref,) = rest
    i = pl.program_id(2)
    row = lax.broadcasted_iota(jnp.int32, (TQ, TQ), 0)
    col = lax.broadcasted_iota(jnp.int32, (TQ, TQ), 1)
    rc = row - col
    d = q_ref.shape[-1]

    def body(kt, carry):
        ms, ls, accs = carry
        start = pl.multiple_of(kt * TQ, TQ)
        ks = k_ref[0, 0, pl.ds(start, TQ), :]
        vs = v_ref[0, 0, pl.ds(start, TQ), :]
        delta = i - kt
        rel = delta * TQ + rc
        if selected:
            mask = (_dot(selv, exp_ref[kt]) > 0.5) & (rel >= 0)
        else:
            mask = (rel >= 0) & (rel < WINDOW)
        bt = jnp.minimum(delta, n_bias - 1)
        new_m, new_l, new_a = [], [], []
        for h in range(NSA_HPG):
            s = _dot_nt(q_ref[0, 0, h], ks) + bias_ref[h, bt]
            s = jnp.where(mask, s, NEG)
            m_new = jnp.maximum(ms[h], jnp.max(s, axis=-1, keepdims=True))
            alpha = jnp.exp(ms[h] - m_new)
            p = jnp.where(mask, jnp.exp(s - m_new), 0.0)
            new_l.append(alpha * ls[h] + jnp.sum(p, axis=-1, keepdims=True))
            new_a.append(alpha * accs[h] + _dot(p.astype(BF16), vs))
            new_m.append(m_new)
        return tuple(new_m), tuple(new_l), tuple(new_a)

    init = (tuple(jnp.full((TQ, 1), NEG, F32) for _ in range(NSA_HPG)),
            tuple(jnp.zeros((TQ, 1), F32) for _ in range(NSA_HPG)),
            tuple(jnp.zeros((TQ, d), F32) for _ in range(NSA_HPG)))
    lo = 0 if selected else jnp.maximum(i - WINDOW // TQ, 0)
    _, ls, accs = lax.fori_loop(lo, i + 1, body, init)
    for h in range(NSA_HPG):
        o_ref[0, 0, h] = accs[h] / ls[h]


def _flash_attention(q, k, v, bias_t, sel=None, expand=None):
    b, g, hg, t, d = q.shape
    n_bias = bias_t.shape[1]
    selected = sel is not None
    in_specs = [
        pl.BlockSpec((1, 1, hg, TQ, d), lambda bi, gi, i: (bi, gi, 0, i, 0)),
        pl.BlockSpec((1, 1, t, d), lambda bi, gi, i: (bi, gi, 0, 0)),
        pl.BlockSpec((1, 1, t, d), lambda bi, gi, i: (bi, gi, 0, 0)),
        pl.BlockSpec((hg, n_bias, TQ, TQ), lambda bi, gi, i: (gi, 0, 0, 0)),
    ]
    args = [q, k, v, bias_t]
    if selected:
        in_specs += [pl.BlockSpec((1, 1, TQ, LANES), lambda bi, gi, i: (bi, gi, i, 0)),
                     pl.BlockSpec(expand.shape, lambda bi, gi, i: (0, 0, 0))]
        args += [sel, expand]
    return pl.pallas_call(
        functools.partial(_flash_body, selected=selected, n_bias=n_bias),
        grid=(b, g, t // TQ),
        in_specs=in_specs,
        out_specs=pl.BlockSpec((1, 1, hg, TQ, d), lambda bi, gi, i: (bi, gi, 0, i, 0)),
        out_shape=jax.ShapeDtypeStruct((b, g, hg, t, d), F32),
        compiler_params=_cparams(("parallel", "parallel", "parallel")),
        name="sel_attention" if selected else "win_attention",
    )(*args)


def _sb_body(q_ref, k_ref, v_ref, uo_ref, o_ref):
    i = pl.program_id(2)
    q = q_ref[0, 0]
    uo = uo_ref[...]
    d = q.shape[-1]
    row = lax.broadcasted_iota(jnp.int32, (TQ, TQ), 0)
    col = lax.broadcasted_iota(jnp.int32, (TQ, TQ), 1)
    rc = row - col

    def cond(st):
        kt, _, _, cmax = st
        return (kt >= 0) & (cmax > SB_SKIP_LOG)

    def body(st):
        kt, c, acc, _ = st
        start = pl.multiple_of(kt * TQ, TQ)
        ks = k_ref[0, 0, pl.ds(start, TQ), :]
        vs = v_ref[0, 0, pl.ds(start, TQ), :]
        z = _dot_nt(q, ks)
        mask = ((i - kt) * TQ + rc) > 0
        lb = jnp.minimum(z, 0.0) - jnp.log1p(jnp.exp(-jnp.abs(z)))
        lr = jnp.where(mask, lb - z, 0.0)
        hi, lo = _split_bf16(lr)
        sfx = _dot(hi, uo) + _dot(lo, uo)
        a = jnp.where(mask, jnp.exp(lb + sfx[:, :TQ] + c), 0.0)
        acc = acc + _dot(a.astype(BF16), vs)
        c = c + sfx[:, TQ:]
        return kt - 1, c, acc, jnp.max(c)

    st = (i, jnp.zeros((TQ, TQ), F32), jnp.zeros((TQ, d), F32), jnp.float32(0.0))
    _, _, acc, _ = lax.while_loop(cond, body, st)
    o_ref[0, 0] = acc


def _stick_breaking(q, k, v):
    b, h, t, d = q.shape
    tri = np.triu(np.ones((TQ, TQ), np.float32), 0).T - np.eye(TQ, dtype=np.float32)
    uo = jnp.asarray(np.concatenate([tri, np.ones((TQ, TQ), np.float32)], axis=1), BF16)
    return pl.pallas_call(
        _sb_body,
        grid=(b, h, t // TQ),
        in_specs=[
            pl.BlockSpec((1, 1, TQ, d), lambda bi, hi, i: (bi, hi, i, 0)),
            pl.BlockSpec((1, 1, t, d), lambda bi, hi, i: (bi, hi, 0, 0)),
            pl.BlockSpec((1, 1, t, d), lambda bi, hi, i: (bi, hi, 0, 0)),
            pl.BlockSpec((TQ, 2 * TQ), lambda bi, hi, i: (0, 0)),
        ],
        out_specs=pl.BlockSpec((1, 1, TQ, d), lambda bi, hi, i: (bi, hi, i, 0)),
        out_shape=jax.ShapeDtypeStruct((b, h, t, d), F32),
        compiler_params=_cparams(("parallel", "parallel", "parallel")),
        name="stick_breaking",
    )(q, k, v, uo)


def _merge_body(oc_ref, os_ref, ow_ref, osb_ref, gbr_ref, ga_ref, gb_ref, e_ref, wn_ref, wsb_ref, o_ref):
    gate = jax.nn.sigmoid(gbr_ref[...].astype(F32))
    hi, lo = _split_bf16(gate)
    e = e_ref[...]
    gexp = _dot(hi, e) + _dot(lo, e)
    o_nsa = (gexp[:, :NSA_Q] * oc_ref[...] + gexp[:, NSA_Q:2 * NSA_Q] * os_ref[...]
             + gexp[:, 2 * NSA_Q:] * ow_ref[...])
    a = _dot(o_nsa.astype(BF16), wn_ref[...])
    bm = _dot(osb_ref[...].astype(BF16), wsb_ref[...])
    merged = jax.nn.sigmoid(ga_ref[...].astype(F32)) * a + jax.nn.sigmoid(gb_ref[...].astype(F32)) * bm
    o_ref[...] = merged.astype(o_ref.dtype)


def _merge(o_c, o_s, o_w, o_sb, proj, col_gbr, col_ga, col_gb, w_nsa, w_sb, tm=256):
    n = o_c.shape[0]
    dm = w_nsa.shape[1]
    e = np.zeros((LANES, 3 * NSA_Q), np.float32)
    for j in range(3 * NSA_HEADS):
        e[j, j * HEAD_DIM:(j + 1) * HEAD_DIM] = 1.0
    row = lambda i: (i, 0)
    full = lambda i: (0, 0)
    return pl.pallas_call(
        _merge_body,
        grid=(n // tm,),
        in_specs=[
            pl.BlockSpec((tm, NSA_Q), row), pl.BlockSpec((tm, NSA_Q), row), pl.BlockSpec((tm, NSA_Q), row),
            pl.BlockSpec((tm, SB_W), row),
            pl.BlockSpec((tm, LANES), lambda i: (i, col_gbr // LANES)),
            pl.BlockSpec((tm, dm), lambda i: (i, col_ga // dm)),
            pl.BlockSpec((tm, dm), lambda i: (i, col_gb // dm)),
            pl.BlockSpec((LANES, 3 * NSA_Q), full),
            pl.BlockSpec((NSA_Q, dm), full), pl.BlockSpec((SB_W, dm), full),
        ],
        out_specs=pl.BlockSpec((tm, dm), row),
        out_shape=jax.ShapeDtypeStruct((n, dm), BF16),
        compiler_params=_cparams(("parallel",)),
        name="gated_merge",
    )(o_c, o_s, o_w, o_sb, proj, proj, proj, jnp.asarray(e, BF16), w_nsa.astype(BF16), w_sb.astype(BF16))


def _peer_scores_body(wqt_ref, xt_ref, keys_ref, s_ref):
    qt = _dot(wqt_ref[...], xt_ref[...])
    c = keys_ref.shape[-1]
    for ch in range(keys_ref.shape[0]):
        qh, ql = _split_bf16(qt[ch * c:(ch + 1) * c])
        kh, kl = _split_bf16(keys_ref[ch])
        s_ref[ch * PEER_KEYS:(ch + 1) * PEER_KEYS, :] = _dot(kh, qh) + _dot(kh, ql) + _dot(kl, qh)


def _peer_scores(wq_t, x_t, keys, tn=512):
    rq, d = wq_t.shape
    n = x_t.shape[1]
    nch, nk, c = keys.shape
    return pl.pallas_call(
        _peer_scores_body,
        grid=(n // tn,),
        in_specs=[pl.BlockSpec((rq, d), lambda i: (0, 0)), pl.BlockSpec((d, tn), lambda i: (0, i)),
                  pl.BlockSpec((nch, nk, c), lambda i: (0, 0, 0))],
        out_specs=pl.BlockSpec((nch * nk, tn), lambda i: (0, i)),
        out_shape=jax.ShapeDtypeStruct((nch * nk, n), F32),
        compiler_params=_cparams(("parallel",)),
        name="peer_scores",
    )(wq_t, x_t, keys)


def _top_rows(x, k):
    r, l = x.shape
    ridx = lax.broadcasted_iota(jnp.int32, (r, l), 0).astype(F32)
    kidx = lax.broadcasted_iota(jnp.int32, (k, l), 0)
    rank = jnp.full((r, l), float(k), F32)
    vals = jnp.zeros((k, l), F32)
    cur = x
    for j in range(k):
        m = jnp.max(cur, axis=0, keepdims=True)
        idx = jnp.min(jnp.where(cur == m, ridx, float(r)), axis=0, keepdims=True)
        pick = ridx == idx
        rank = jnp.where(pick, float(j), rank)
        cur = jnp.where(pick, -jnp.inf, cur)
        vals = jnp.where(kidx == j, m, vals)
    return vals, rank


def _peer_route_body(s_ref, lim0_ref, c0_ref, rank1_ref, e1_ref):
    k = PEER_TOPK
    tn = s_ref.shape[-1]
    kidx = lax.broadcasted_iota(jnp.int32, (k, tn), 0)
    flat = lax.broadcasted_iota(jnp.int32, (k * k, tn), 0)
    reachable = (jnp.right_shift(flat, int(math.log2(k))) + 1) * (jnp.bitwise_and(flat, k - 1) + 1) <= k
    for h in range(PEER_HEADS):
        s0 = s_ref[(2 * h) * PEER_KEYS:(2 * h + 1) * PEER_KEYS, :]
        s1 = s_ref[(2 * h + 1) * PEER_KEYS:(2 * h + 2) * PEER_KEYS, :]
        a, rank0 = _top_rows(s0, k)
        b, rank1 = _top_rows(s1, k)
        cand = jnp.concatenate([a[r:r + 1] + b for r in range(k)], axis=0)
        cand = jnp.where(reachable, cand, -jnp.inf)
        best, crank = _top_rows(cand, k)
        chosen = jnp.where(crank < float(k), 1.0, 0.0)
        e = chosen * jnp.exp(jnp.minimum(cand - best[0:1], 0.0))
        z = jnp.sum(e, axis=0, keepdims=True)
        cnt = jnp.zeros((k, tn), F32)
        for r in range(k):
            n_r = jnp.sum(chosen[r * k:(r + 1) * k], axis=0, keepdims=True)
            cnt = jnp.where(kidx == r, n_r, cnt)
        lim0 = jnp.zeros_like(s0)
        for r in range(k):
            lim0 = jnp.where(rank0 == float(r), cnt[r:r + 1], lim0)
        lim0_ref[h] = lim0
        c0_ref[h] = jnp.exp(s0 - a[0:1]) / z
        rank1_ref[h] = rank1
        e1_ref[h] = jnp.exp(s1 - b[0:1])


def _peer_route(s_t, tn=256):
    rows, n = s_t.shape
    shp = jax.ShapeDtypeStruct((PEER_HEADS, PEER_KEYS, n), F32)
    spec = pl.BlockSpec((PEER_HEADS, PEER_KEYS, tn), lambda i: (0, 0, i))
    return pl.pallas_call(
        _peer_route_body,
        grid=(n // tn,),
        in_specs=[pl.BlockSpec((rows, tn), lambda i: (0, i))],
        out_specs=[spec, spec, spec, spec],
        out_shape=[shp, shp, shp, shp],
        compiler_params=_cparams(("parallel",)),
        name="peer_route",
    )(s_t)


def _peer_main_body(u_ref, vt_ref, xt_ref, lim0_ref, c0_ref, rank1_ref, e1_ref, o_ref, *, te):
    et = pl.program_id(1)

    @pl.when(et == 0)
    def _():
        o_ref[...] = jnp.zeros_like(o_ref)

    g = jax.nn.gelu(_dot(u_ref[...], xt_ref[...]))
    n_i = te // PEER_KEYS
    pieces = []
    for ii in range(n_i):
        i0 = et * n_i + ii
        s = jnp.zeros((PEER_KEYS, g.shape[1]), F32)
        for h in range(PEER_HEADS):
            lim = lim0_ref[h, pl.ds(i0, 1), :]
            cc = c0_ref[h, pl.ds(i0, 1), :]
            s = s + jnp.where(rank1_ref[h] < lim, e1_ref[h] * cc, 0.0)
        pieces.append((s * g[ii * PEER_KEYS:(ii + 1) * PEER_KEYS]).astype(BF16))
    w = jnp.concatenate(pieces, axis=0) if n_i > 1 else pieces[0]
    o_ref[...] += _dot(vt_ref[...], w)


def _peer_main(u, v_t, x_t, lim0, c0, rank1, e1, tn=512, te=512):
    n_exp, d = u.shape
    n = x_t.shape[1]
    tab = pl.BlockSpec((PEER_HEADS, PEER_KEYS, tn), lambda i, j: (0, 0, i))
    return pl.pallas_call(
        functools.partial(_peer_main_body, te=te),
        grid=(n // tn, n_exp // te),
        in_specs=[pl.BlockSpec((te, d), lambda i, j: (j, 0)), pl.BlockSpec((d, te), lambda i, j: (0, j)),
                  pl.BlockSpec((d, tn), lambda i, j: (0, i)), tab, tab, tab, tab],
        out_specs=pl.BlockSpec((d, tn), lambda i, j: (0, i)),
        out_shape=jax.ShapeDtypeStruct((d, n), F32),
        compiler_params=_cparams(("parallel", "arbitrary"), vmem_mb=56),
        name="peer_experts",
    )(u, v_t, x_t, lim0, c0, rank1, e1)


def kernel(x, attn_norm_g, w_in, cmp_k_pe, cmp_k_w1, cmp_k_w2, cmp_v_pe, cmp_v_w1, cmp_v_w2, rel_bias_table,
           w_branch_nsa, w_branch_sb, w_out, ffn_norm_g, peer_w_q, peer_sub_keys, peer_u, peer_v, final_norm_g):
    h = x
    for l in range(attn_norm_g.shape[0]):
        h = _layer(h, attn_norm_g[l], w_in[l], cmp_k_pe[l], cmp_k_w1[l], cmp_k_w2[l], cmp_v_pe[l], cmp_v_w1[l],
                   cmp_v_w2[l], rel_bias_table, w_branch_nsa[l], w_branch_sb[l], w_out[l], ffn_norm_g[l],
                   peer_w_q[l], peer_sub_keys[l], peer_u[l], peer_v[l],
                   final_norm_g if l == attn_norm_g.shape[0] - 1 else None)
    return h


def _layer(h, attn_g, w_in, ck_pe, ck_w1, ck_w2, cv_pe, cv_w1, cv_w2, rel_table, w_br_nsa, w_br_sb, w_out,
           ffn_g, pq, psk, pu, pv, final_g):
    b, t, dm = h.shape
    n = b * t
    g, hg, d = NSA_GROUPS, NSA_HPG, HEAD_DIM
    nq = t // TQ
    x2 = h.reshape(n, dm)

    o_gbr = NSA_Q + 6 * NSA_KV
    n_gbr = 3 * NSA_HEADS
    o_qb = o_gbr + n_gbr
    o_ga = o_qb + 3 * SB_W
    tn_in = 768
    packed = w_in.shape[1]
    packed_pad = -(-packed // tn_in) * tn_in
    w_pack = jnp.concatenate([w_in[:, o_ga:], w_in[:, :o_gbr], w_in[:, o_qb:o_ga], w_in[:, o_gbr:o_qb],
                              jnp.zeros((dm, packed_pad - packed), w_in.dtype)], axis=1).astype(BF16)
    a = _rmsnorm(x2, attn_g, BF16)
    proj = _matmul(a, w_pack, 512, tn_in, BF16, name="in_proj")
    col_ga, col_gb = 0, dm
    col_qn = 2 * dm
    col_kv = col_qn + NSA_Q
    col_qb = col_kv + 6 * NSA_KV
    col_gbr = col_qb + 3 * SB_W

    scale = HEAD_DIM ** -0.5

    def heads_nsa(z):
        return z.reshape(b, t, g, hg, d).transpose(0, 2, 3, 1, 4)

    def heads_kv(z):
        return z.reshape(b, t, g, d).transpose(0, 2, 1, 3)

    def heads_sb(z):
        return z.reshape(b, t, SB_HEADS, d).transpose(0, 2, 1, 3)

    q_n = heads_nsa(proj[:, col_qn:col_qn + NSA_Q] * jnp.asarray(scale, BF16))
    kv = [heads_kv(proj[:, col_kv + j * NSA_KV:col_kv + (j + 1) * NSA_KV]) for j in range(6)]
    kc_tok, vc_tok, ks, vs, kw, vw = kv
    q_b = heads_sb(proj[:, col_qb:col_qb + SB_W] * jnp.asarray(scale, BF16))
    k_b = heads_sb(proj[:, col_qb + SB_W:col_qb + 2 * SB_W])
    v_b = heads_sb(proj[:, col_qb + 2 * SB_W:col_qb + 3 * SB_W])

    n_cmp = (t - CMP_LEN) // CMP_STRIDE + 1
    n_chunk = t // CMP_STRIDE
    ncp = -(-n_cmp // LANES) * LANES
    reps = CMP_LEN // CMP_STRIDE

    def blocks(tok):
        ch = tok.reshape(b, g, n_chunk, CMP_STRIDE * d)
        ch = jnp.pad(ch, ((0, 0), (0, 0), (0, ncp + reps - 1 - n_chunk), (0, 0)))
        blk = jnp.concatenate([ch[:, :, r:r + ncp] for r in range(reps)], axis=-1)
        return blk.reshape(b * g * ncp, CMP_LEN * d)

    kc_blk = _compress(blocks(kc_tok), ck_pe.reshape(1, CMP_LEN * d), ck_w1, ck_w2, ncp).reshape(b, g, ncp, d)
    vc_blk = _compress(blocks(vc_tok), cv_pe.reshape(1, CMP_LEN * d), cv_w1, cv_w2, ncp).reshape(b, g, ncp, d)

    tpos = jnp.arange(t, dtype=jnp.int32)
    dist_c = tpos[:, None] - (jnp.arange(ncp, dtype=jnp.int32)[None, :] * CMP_STRIDE + CMP_LEN - 1)
    bias_c = _bias_expand(rel_table, _rel_bucket(dist_c), TQ)
    n_bias = min(nq, -(-(REL_MAX_DIST + TQ - 1) // TQ) + 1)
    r_i = jnp.arange(TQ, dtype=jnp.int32)
    dist_t = (jnp.arange(n_bias, dtype=jnp.int32)[:, None, None] * TQ + r_i[None, :, None] - r_i[None, None, :])
    bias_t = _bias_expand(rel_table, _rel_bucket(dist_t.reshape(n_bias * TQ, TQ)), TQ)
    bias_t = bias_t.reshape(NSA_HEADS, n_bias, TQ, TQ)

    n_sel = t // SEL_LEN
    c_start = np.arange(ncp) * CMP_STRIDE
    s_start = np.arange(LANES) * SEL_LEN
    overlap = np.maximum(np.minimum(c_start[:, None] + CMP_LEN, s_start[None, :] + SEL_LEN)
                         - np.maximum(c_start[:, None], s_start[None, :]), 0).astype(np.float32) / CMP_LEN
    overlap[n_cmp:, :] = 0.0
    overlap[:, n_sel:] = 0.0
    o_c, sel = _cmp_attention(q_n, kc_blk, vc_blk, bias_c, jnp.asarray(overlap, BF16), n_cmp,
                              min(SEL_TOPK, n_sel))
    key_blk = np.arange(t) // SEL_LEN
    expand = (np.arange(LANES)[None, :, None] == key_blk.reshape(nq, 1, TQ)).astype(np.float32)
    o_s = _flash_attention(q_n, ks, vs, bias_t, sel, jnp.asarray(expand, BF16))
    o_w = _flash_attention(q_n, kw, vw, bias_t)
    o_sb = _stick_breaking(q_b, k_b, v_b)

    def tokens(o):
        o = o.reshape(b, -1, t, d)
        return o.transpose(0, 2, 1, 3).reshape(n, -1)

    merged = _merge(tokens(o_c), tokens(o_s), tokens(o_w), tokens(o_sb), proj, col_gbr, col_ga, col_gb,
                    w_br_nsa, w_br_sb)
    h1 = _matmul(merged, w_out.astype(BF16), 512, 1024, F32, res=x2, name="out_proj")

    xn_t = _rmsnorm(h1, ffn_g, BF16).T
    keys = psk.reshape(PEER_HEADS * 2, PEER_KEYS, -1)
    s_t = _peer_scores(pq.T.astype(BF16), xn_t, keys)
    lim0, c0, rank1, e1 = _peer_route(s_t)
    ffn_t = _peer_main(pu.astype(BF16), pv.T.astype(BF16), xn_t, lim0, c0, rank1, e1)
    ffn = ffn_t.T
    if final_g is None:
        return (h1 + ffn).reshape(b, t, dm)
    return _add_rmsnorm(h1, ffn, final_g).reshape(b, t, dm)
```

```python
import functools
import math

import jax
import jax.numpy as jnp
import numpy as np
from jax import lax
from jax.experimental import pallas as pl
from jax.experimental.pallas import tpu as pltpu

F32 = jnp.float32
BF16 = jnp.bfloat16

HEAD_DIM = 64
NSA_HEADS = 16
NSA_GROUPS = 4
NSA_HPG = NSA_HEADS // NSA_GROUPS
SB_HEADS = 16
CMP_LEN = 32
CMP_STRIDE = 16
SEL_LEN = 64
SEL_TOPK = 16
N_LOCAL_SEL = 2
SEL_FORCE_BONUS = 1e4
WINDOW = 512
REL_BUCKETS = 32
REL_MAX_DIST = 1024
PEER_HEADS = 8
PEER_KEYS = 128
PEER_TOPK = 16
EPS = 1e-6
NEG = -1e30

NSA_Q = NSA_HEADS * HEAD_DIM
NSA_KV = NSA_GROUPS * HEAD_DIM
SB_W = SB_HEADS * HEAD_DIM

LANES = 128
SUBLANES = 8
TQ = 128
SB_HEADS_PER_STEP = 8
SB_SKIP_LOG = -110.0

_NT = (((1,), (1,)), ((), ()))


def _cparams(sem, vmem_mb=48):
    return pltpu.CompilerParams(dimension_semantics=sem, vmem_limit_bytes=vmem_mb * 1024 * 1024)


def _dot(a, b):
    return jnp.dot(a, b, preferred_element_type=F32)


def _dot_nt(a, b):
    return lax.dot_general(a, b, _NT, preferred_element_type=F32)


def _split_bf16(x):
    hi = x.astype(BF16)
    lo = (x - hi.astype(F32)).astype(BF16)
    return hi, lo


def _fold_rows(x, op):
    r, l = x.shape
    return op(x.reshape(r // SUBLANES, SUBLANES, l), axis=0)


def _rmsnorm_body(x_ref, g_ref, o_ref):
    x = x_ref[...]
    y = x * lax.rsqrt(jnp.mean(x * x, axis=-1, keepdims=True) + EPS)
    o_ref[...] = (y * g_ref[...]).astype(o_ref.dtype)


def _rmsnorm(x, g, out_dtype, tm=512):
    n, d = x.shape
    return pl.pallas_call(
        _rmsnorm_body,
        grid=(n // tm,),
        in_specs=[pl.BlockSpec((tm, d), lambda i: (i, 0)), pl.BlockSpec((1, d), lambda i: (0, 0))],
        out_specs=pl.BlockSpec((tm, d), lambda i: (i, 0)),
        out_shape=jax.ShapeDtypeStruct((n, d), out_dtype),
        compiler_params=_cparams(("parallel",)),
        name="rmsnorm",
    )(x, g.reshape(1, d))


def _add_rmsnorm_body(x_ref, y_ref, g_ref, o_ref):
    x = x_ref[...] + y_ref[...]
    y = x * lax.rsqrt(jnp.mean(x * x, axis=-1, keepdims=True) + EPS)
    o_ref[...] = (y * g_ref[...]).astype(o_ref.dtype)


def _add_rmsnorm(x, y, g, tm=512):
    n, d = x.shape
    return pl.pallas_call(
        _add_rmsnorm_body,
        grid=(n // tm,),
        in_specs=[pl.BlockSpec((tm, d), lambda i: (i, 0)), pl.BlockSpec((tm, d), lambda i: (i, 0)),
                  pl.BlockSpec((1, d), lambda i: (0, 0))],
        out_specs=pl.BlockSpec((tm, d), lambda i: (i, 0)),
        out_shape=jax.ShapeDtypeStruct((n, d), F32),
        compiler_params=_cparams(("parallel",)),
        name="add_rmsnorm",
    )(x, y, g.reshape(1, d))


def _mm_body(a_ref, w_ref, o_ref):
    o_ref[...] = _dot(a_ref[...], w_ref[...]).astype(o_ref.dtype)


def _mm_res_body(a_ref, w_ref, r_ref, o_ref):
    o_ref[...] = (r_ref[...] + _dot(a_ref[...], w_ref[...])).astype(o_ref.dtype)


def _matmul(a, w, tm, tn, out_dtype, res=None, name="matmul"):
    m, k = a.shape
    nc = w.shape[1]
    in_specs = [pl.BlockSpec((tm, k), lambda j, i: (i, 0)), pl.BlockSpec((k, tn), lambda j, i: (0, j))]
    args = [a, w]
    body = _mm_body
    if res is not None:
        in_specs.append(pl.BlockSpec((tm, tn), lambda j, i: (i, j)))
        args.append(res)
        body = _mm_res_body
    return pl.pallas_call(
        body,
        grid=(nc // tn, m // tm),
        in_specs=in_specs,
        out_specs=pl.BlockSpec((tm, tn), lambda j, i: (i, j)),
        out_shape=jax.ShapeDtypeStruct((m, nc), out_dtype),
        compiler_params=_cparams(("parallel", "parallel")),
        name=name,
    )(*args)


def _rel_bucket(dist):
    dist = jnp.maximum(dist, 0)
    n_exact = REL_BUCKETS // 2
    d_f = jnp.maximum(dist, 1).astype(jnp.float32)
    large = n_exact + (jnp.log(d_f / n_exact) / math.log(REL_MAX_DIST / n_exact)
                       * (REL_BUCKETS - n_exact)).astype(jnp.int32)
    large = jnp.minimum(large, REL_BUCKETS - 1)
    return jnp.where(dist < n_exact, dist, large)


def _bias_expand_body(tab_ref, bkt_ref, o_ref):
    h = pl.program_id(1)
    bkt = bkt_ref[0]
    acc = jnp.full(bkt.shape, tab_ref[0, h], F32)
    for k in range(1, REL_BUCKETS):
        acc = jnp.where(bkt == k, tab_ref[k, h], acc)
    o_ref[0, 0] = acc


def _bias_expand(table, bucket):
    r, rows, _ = bucket.shape
    return pl.pallas_call(
        _bias_expand_body,
        grid=(r, NSA_HEADS),
        in_specs=[pl.BlockSpec(memory_space=pltpu.SMEM), pl.BlockSpec((1, rows, TQ), lambda i, h: (i, 0, 0))],
        out_specs=pl.BlockSpec((1, 1, rows, TQ), lambda i, h: (h // NSA_HPG, i, 0, h % NSA_HPG)),
        out_shape=jax.ShapeDtypeStruct((NSA_GROUPS, r, rows, NSA_HPG * TQ), F32),
        compiler_params=_cparams(("parallel", "parallel")),
        name="bias_expand",
    )(table, bucket)


def _top_rows(x, k):
    r, l = x.shape
    ridx = lax.broadcasted_iota(jnp.int32, (r, l), 0).astype(F32)
    kidx = lax.broadcasted_iota(jnp.int32, (k, l), 0)
    rank = jnp.full((r, l), float(k), F32)
    vals = jnp.zeros((k, l), F32)
    cur = x
    for j in range(k):
        m = jnp.max(cur, axis=0, keepdims=True)
        idx = jnp.min(jnp.where(cur == m, ridx, float(r)), axis=0, keepdims=True)
        pick = ridx == idx
        rank = jnp.where(pick, float(j), rank)
        cur = jnp.where(pick, -jnp.inf, cur)
        vals = jnp.where(kidx == j, m, vals)
    return vals, rank


def _compress_body(blk_ref, pe_ref, w1_ref, w2_ref, o_ref):
    blk = (blk_ref[...].astype(F32) + pe_ref[...]).astype(BF16)
    h = jax.nn.gelu(_dot(blk, w1_ref[...]))
    o_ref[...] = _dot(h.astype(BF16), w2_ref[...]).astype(o_ref.dtype)


def _compress(blk, pe_flat, w1, w2, tr):
    r, kd = blk.shape
    hid = w1.shape[1]
    d = w2.shape[1]
    return pl.pallas_call(
        _compress_body,
        grid=(r // tr,),
        in_specs=[pl.BlockSpec((tr, kd), lambda i: (i, 0)), pl.BlockSpec((1, kd), lambda i: (0, 0)),
                  pl.BlockSpec((kd, hid), lambda i: (0, 0)), pl.BlockSpec((hid, d), lambda i: (0, 0))],
        out_specs=pl.BlockSpec((tr, d), lambda i: (i, 0)),
        out_shape=jax.ShapeDtypeStruct((r, d), BF16),
        compiler_params=_cparams(("parallel",)),
        name="compress_mlp",
    )(blk, pe_flat, w1.astype(BF16), w2.astype(BF16))


def _cmp_attn_body(q_ref, kc_ref, vct_ref, bias_ref, ovt_ref, o_ref, sel_ref, *, n_cmp, n_top):
    i = pl.program_id(2)
    q = q_ref[0, 0, 0]
    kc = kc_ref[0, 0]
    ncp, w = kc.shape[0], q.shape[1]
    blk = lax.broadcasted_iota(jnp.int32, (ncp, w), 0)
    t = i * TQ + jnp.bitwise_and(lax.broadcasted_iota(jnp.int32, (ncp, w), 1), TQ - 1)
    valid = ((t - CMP_STRIDE * blk - (CMP_LEN - 1)) >= 0) & (blk < n_cmp)
    s = jnp.where(valid, _dot(kc, q) + bias_ref[0, 0], NEG)
    m = jnp.max(s, axis=0, keepdims=True)
    e = jnp.where(valid, jnp.exp(s - m), 0.0)
    l = jnp.sum(e, axis=0, keepdims=True)
    p = e / jnp.maximum(l, 1e-30)
    o_ref[0, 0, 0] = _dot(vct_ref[0, 0], p.astype(BF16))
    psum = p[:, :TQ]
    for h in range(1, NSA_HPG):
        psum = psum + p[:, h * TQ:(h + 1) * TQ]
    hi, lo = _split_bf16(psum)
    ovt = ovt_ref[...]
    imp = _dot(ovt, hi) + _dot(ovt, lo)
    nsp = imp.shape[0]
    j = lax.broadcasted_iota(jnp.int32, (nsp, TQ), 0)
    cur = jnp.right_shift(i * TQ + lax.broadcasted_iota(jnp.int32, (nsp, TQ), 1), int(math.log2(SEL_LEN)))
    gap = cur - j
    forced = (j == 0) | ((gap >= 0) & (gap < N_LOCAL_SEL))
    allowed = j <= cur
    score = jnp.where(allowed, imp + SEL_FORCE_BONUS * forced.astype(F32), -jnp.inf)
    _, rank = _top_rows(score, n_top)
    sel = jnp.where((rank < float(n_top)) & allowed, 1.0, 0.0)
    if nsp < LANES:
        sel = jnp.concatenate([sel, jnp.zeros((LANES - nsp, TQ), F32)], axis=0)
    sel_ref[0, 0, 0] = sel.astype(sel_ref.dtype)


def _cmp_attention(q, kc, vc_t, bias_c, overlap_t, n_cmp, n_top):
    b, g, nq, d, w = q.shape
    ncp = kc.shape[2]
    return pl.pallas_call(
        functools.partial(_cmp_attn_body, n_cmp=n_cmp, n_top=n_top),
        grid=(b, g, nq),
        in_specs=[
            pl.BlockSpec((1, 1, 1, d, w), lambda bi, gi, i: (bi, gi, i, 0, 0)),
            pl.BlockSpec((1, 1, ncp, d), lambda bi, gi, i: (bi, gi, 0, 0)),
            pl.BlockSpec((1, 1, d, ncp), lambda bi, gi, i: (bi, gi, 0, 0)),
            pl.BlockSpec((1, 1, ncp, w), lambda bi, gi, i: (gi, i, 0, 0)),
            pl.BlockSpec(overlap_t.shape, lambda bi, gi, i: (0, 0)),
        ],
        out_specs=[
            pl.BlockSpec((1, 1, 1, d, w), lambda bi, gi, i: (bi, gi, i, 0, 0)),
            pl.BlockSpec((1, 1, 1, LANES, TQ), lambda bi, gi, i: (bi, gi, i, 0, 0)),
        ],
        out_shape=[jax.ShapeDtypeStruct((b, g, nq, d, w), F32), jax.ShapeDtypeStruct((b, g, nq, LANES, TQ), BF16)],
        compiler_params=_cparams(("parallel", "parallel", "parallel")),
        name="cmp_attention",
    )(q, kc, vc_t, bias_c, overlap_t)


def _nsa_attn_body(q_ref, k_ref, vt_ref, bias_ref, *rest, selected, n_bias):
    if selected:
        sel_ref, exp_ref, o_ref, s_ref = rest
        sel4 = jnp.concatenate([sel_ref[0, 0, 0]] * NSA_HPG, axis=1)
    else:
        o_ref, s_ref = rest
    i = pl.program_id(2)
    q = q_ref[0, 0, 0]
    d, w = q.shape
    key = lax.broadcasted_iota(jnp.int32, (TQ, w), 0)
    qk = jnp.bitwise_and(lax.broadcasted_iota(jnp.int32, (TQ, w), 1), TQ - 1) - key
    lo = 0 if selected else jnp.maximum(i - WINDOW // TQ, 0)

    def scores(kt, mx):
        ks = k_ref[0, 0, pl.ds(pl.multiple_of(kt * TQ, TQ), TQ), :]
        delta = i - kt
        rel = delta * TQ + qk
        if selected:
            ok = (_dot(exp_ref[kt], sel4) > 0.5) & (rel >= 0)
        else:
            ok = (rel >= 0) & (rel < WINDOW)
        s = jnp.where(ok, _dot(ks, q) + bias_ref[0, jnp.minimum(delta, n_bias - 1)], NEG)
        s_ref[kt] = s
        return jnp.maximum(mx, _fold_rows(s, jnp.max))

    mx = lax.fori_loop(lo, i + 1, scores, jnp.full((SUBLANES, w), NEG, F32))
    m = jnp.max(mx, axis=0, keepdims=True)

    def weights(kt, carry):
        l8, acc = carry
        p = jnp.exp(s_ref[kt] - m)
        return l8 + _fold_rows(p, jnp.sum), acc + _dot(vt_ref[0, 0, kt], p.astype(BF16))

    l8, acc = lax.fori_loop(lo, i + 1, weights, (jnp.zeros((SUBLANES, w), F32), jnp.zeros((d, w), F32)))
    o_ref[0, 0, 0] = acc / jnp.sum(l8, axis=0, keepdims=True)


def _nsa_attention(q, k, v_t, bias_t, sel=None, expand_t=None):
    b, g, nq, d, w = q.shape
    t = k.shape[2]
    n_bias = bias_t.shape[1]
    selected = sel is not None
    in_specs = [
        pl.BlockSpec((1, 1, 1, d, w), lambda bi, gi, i: (bi, gi, i, 0, 0)),
        pl.BlockSpec((1, 1, t, d), lambda bi, gi, i: (bi, gi, 0, 0)),
        pl.BlockSpec((1, 1, nq, d, TQ), lambda bi, gi, i: (bi, gi, 0, 0, 0)),
        pl.BlockSpec((1, n_bias, TQ, w), lambda bi, gi, i: (gi, 0, 0, 0)),
    ]
    args = [q, k, v_t, bias_t]
    if selected:
        in_specs += [pl.BlockSpec((1, 1, 1, LANES, TQ), lambda bi, gi, i: (bi, gi, i, 0, 0)),
                     pl.BlockSpec(expand_t.shape, lambda bi, gi, i: (0, 0, 0))]
        args += [sel, expand_t]
    return pl.pallas_call(
        functools.partial(_nsa_attn_body, selected=selected, n_bias=n_bias),
        grid=(b, g, nq),
        in_specs=in_specs,
        out_specs=pl.BlockSpec((1, 1, 1, d, w), lambda bi, gi, i: (bi, gi, i, 0, 0)),
        out_shape=jax.ShapeDtypeStruct((b, g, nq, d, w), F32),
        scratch_shapes=[pltpu.VMEM((nq, TQ, w), F32)],
        compiler_params=_cparams(("parallel", "parallel", "parallel")),
        name="sel_attention" if selected else "win_attention",
    )(*args)


def _sb_body(q_ref, kv_ref, uo_ref, o_ref, c_ref):
    i = pl.program_id(2)
    uo = uo_ref[...]
    hb = q_ref.shape[1]
    row = lax.broadcasted_iota(jnp.int32, (TQ, TQ), 0)
    col = lax.broadcasted_iota(jnp.int32, (TQ, TQ), 1)
    rc = row - col
    c_ref[...] = jnp.zeros_like(c_ref)
    o_ref[...] = jnp.zeros_like(o_ref)

    def cond(st):
        kt, cmax = st
        return (kt >= 0) & (cmax > SB_SKIP_LOG)

    def body(st):
        kt, _ = st
        start = pl.multiple_of(kt * TQ, TQ)
        mask = ((i - kt) * TQ + rc) > 0
        cm = None
        for h in range(hb):
            kv = kv_ref[0, h, pl.ds(start, TQ), :]
            z = _dot_nt(q_ref[0, h], kv)
            lb = jnp.minimum(z, 0.0) - jnp.log1p(jnp.exp(-jnp.abs(z)))
            lr = jnp.where(mask, lb - z, 0.0)
            hi, lo = _split_bf16(lr)
            sfx = _dot(hi, uo) + _dot(lo, uo)
            c = c_ref[h]
            a = jnp.where(mask, jnp.exp(lb + sfx[:, :TQ] + c), 0.0)
            o_ref[0, h] += _dot(a.astype(BF16), kv)
            c = c + sfx[:, TQ:]
            c_ref[h] = c
            cm = c if cm is None else jnp.maximum(cm, c)
        return kt - 1, jnp.max(cm)

    lax.while_loop(cond, body, (i, jnp.float32(0.0)))


def _stick_breaking(q_pad, kv):
    b, h, t, d2 = q_pad.shape
    hb = SB_HEADS_PER_STEP
    tri = np.triu(np.ones((TQ, TQ), np.float32), 0).T - np.eye(TQ, dtype=np.float32)
    uo = jnp.asarray(np.concatenate([tri, np.ones((TQ, TQ), np.float32)], axis=1), BF16)
    return pl.pallas_call(
        _sb_body,
        grid=(b, h // hb, t // TQ),
        in_specs=[
            pl.BlockSpec((1, hb, TQ, d2), lambda bi, hi, i: (bi, hi, i, 0)),
            pl.BlockSpec((1, hb, t, d2), lambda bi, hi, i: (bi, hi, 0, 0)),
            pl.BlockSpec((TQ, 2 * TQ), lambda bi, hi, i: (0, 0)),
        ],
        out_specs=pl.BlockSpec((1, hb, TQ, d2), lambda bi, hi, i: (bi, hi, i, 0)),
        out_shape=jax.ShapeDtypeStruct((b, h, t, d2), F32),
        scratch_shapes=[pltpu.VMEM((hb, TQ, TQ), F32)],
        compiler_params=_cparams(("parallel", "parallel", "parallel")),
        name="stick_breaking",
    )(q_pad, kv, uo)


def _merge_body(oc_ref, os_ref, ow_ref, osb_ref, gbr_ref, ga_ref, gb_ref, e_ref, wn_ref, wsb_ref, o_ref):
    gate = jax.nn.sigmoid(gbr_ref[...].astype(F32))
    hi, lo = _split_bf16(gate)
    e = e_ref[...]
    gexp = _dot(hi, e) + _dot(lo, e)
    o_nsa = (gexp[:, :NSA_Q] * oc_ref[...] + gexp[:, NSA_Q:2 * NSA_Q] * os_ref[...]
             + gexp[:, 2 * NSA_Q:] * ow_ref[...])
    a = _dot(o_nsa.astype(BF16), wn_ref[...])
    bm = _dot(osb_ref[...].astype(BF16), wsb_ref[...])
    merged = jax.nn.sigmoid(ga_ref[...].astype(F32)) * a + jax.nn.sigmoid(gb_ref[...].astype(F32)) * bm
    o_ref[...] = merged.astype(o_ref.dtype)


def _merge(o_c, o_s, o_w, o_sb, proj, col_gbr, col_ga, col_gb, w_nsa, w_sb, tm=256):
    n = o_c.shape[0]
    dm = w_nsa.shape[1]
    e = np.zeros((LANES, 3 * NSA_Q), np.float32)
    for j in range(3 * NSA_HEADS):
        e[j, j * HEAD_DIM:(j + 1) * HEAD_DIM] = 1.0
    row = lambda i: (i, 0)
    full = lambda i: (0, 0)
    return pl.pallas_call(
        _merge_body,
        grid=(n // tm,),
        in_specs=[
            pl.BlockSpec((tm, NSA_Q), row), pl.BlockSpec((tm, NSA_Q), row), pl.BlockSpec((tm, NSA_Q), row),
            pl.BlockSpec((tm, SB_W), row),
            pl.BlockSpec((tm, LANES), lambda i: (i, col_gbr // LANES)),
            pl.BlockSpec((tm, dm), lambda i: (i, col_ga // dm)),
            pl.BlockSpec((tm, dm), lambda i: (i, col_gb // dm)),
            pl.BlockSpec((LANES, 3 * NSA_Q), full),
            pl.BlockSpec((NSA_Q, dm), full), pl.BlockSpec((SB_W, dm), full),
        ],
        out_specs=pl.BlockSpec((tm, dm), row),
        out_shape=jax.ShapeDtypeStruct((n, dm), BF16),
        compiler_params=_cparams(("parallel",)),
        name="gated_merge",
    )(o_c, o_s, o_w, o_sb, proj, proj, proj, jnp.asarray(e, BF16), w_nsa.astype(BF16), w_sb.astype(BF16))


def _peer_scores_body(wqt_ref, xt_ref, keys_ref, s_ref):
    qt = _dot(wqt_ref[...], xt_ref[...])
    c = keys_ref.shape[-1]
    for ch in range(keys_ref.shape[0]):
        qh, ql = _split_bf16(qt[ch * c:(ch + 1) * c])
        kh, kl = _split_bf16(keys_ref[ch])
        s_ref[ch * PEER_KEYS:(ch + 1) * PEER_KEYS, :] = _dot(kh, qh) + _dot(kh, ql) + _dot(kl, qh)


def _peer_scores(wq_t, x_t, keys, tn=512):
    rq, d = wq_t.shape
    n = x_t.shape[1]
    nch, nk, c = keys.shape
    return pl.pallas_call(
        _peer_scores_body,
        grid=(n // tn,),
        in_specs=[pl.BlockSpec((rq, d), lambda i: (0, 0)), pl.BlockSpec((d, tn), lambda i: (0, i)),
                  pl.BlockSpec((nch, nk, c), lambda i: (0, 0, 0))],
        out_specs=pl.BlockSpec((nch * nk, tn), lambda i: (0, i)),
        out_shape=jax.ShapeDtypeStruct((nch * nk, n), F32),
        compiler_params=_cparams(("parallel",)),
        name="peer_scores",
    )(wq_t, x_t, keys)


def _peer_route_body(s_ref, lim0_ref, c0_ref, rank1_ref, e1_ref):
    k = PEER_TOPK
    tn = s_ref.shape[-1]
    kidx = lax.broadcasted_iota(jnp.int32, (k, tn), 0)
    flat = lax.broadcasted_iota(jnp.int32, (k * k, tn), 0)
    reachable = (jnp.right_shift(flat, int(math.log2(k))) + 1) * (jnp.bitwise_and(flat, k - 1) + 1) <= k
    for h in range(PEER_HEADS):
        s0 = s_ref[(2 * h) * PEER_KEYS:(2 * h + 1) * PEER_KEYS, :]
        s1 = s_ref[(2 * h + 1) * PEER_KEYS:(2 * h + 2) * PEER_KEYS, :]
        a, rank0 = _top_rows(s0, k)
        b, rank1 = _top_rows(s1, k)
        cand = jnp.concatenate([a[r:r + 1] + b for r in range(k)], axis=0)
        cand = jnp.where(reachable, cand, -jnp.inf)
        best, crank = _top_rows(cand, k)
        chosen = jnp.where(crank < float(k), 1.0, 0.0)
        e = chosen * jnp.exp(jnp.minimum(cand - best[0:1], 0.0))
        z = jnp.sum(e, axis=0, keepdims=True)
        cnt = jnp.zeros((k, tn), F32)
        for r in range(k):
            n_r = jnp.sum(chosen[r * k:(r + 1) * k], axis=0, keepdims=True)
            cnt = jnp.where(kidx == r, n_r, cnt)
        lim0 = jnp.zeros_like(s0)
        for r in range(k):
            lim0 = jnp.where(rank0 == float(r), cnt[r:r + 1], lim0)
        lim0_ref[h] = lim0
        c0_ref[h] = jnp.exp(s0 - a[0:1]) / z
        rank1_ref[h] = rank1
        e1_ref[h] = jnp.exp(s1 - b[0:1])


def _peer_route(s_t, tn=256):
    rows, n = s_t.shape
    shp = jax.ShapeDtypeStruct((PEER_HEADS, PEER_KEYS, n), F32)
    spec = pl.BlockSpec((PEER_HEADS, PEER_KEYS, tn), lambda i: (0, 0, i))
    return pl.pallas_call(
        _peer_route_body,
        grid=(n // tn,),
        in_specs=[pl.BlockSpec((rows, tn), lambda i: (0, i))],
        out_specs=[spec, spec, spec, spec],
        out_shape=[shp, shp, shp, shp],
        compiler_params=_cparams(("parallel",)),
        name="peer_route",
    )(s_t)


def _peer_main_body(u_ref, vt_ref, xt_ref, lim0_ref, c0_ref, rank1_ref, e1_ref, o_ref, *, te):
    et = pl.program_id(1)

    @pl.when(et == 0)
    def _():
        o_ref[...] = jnp.zeros_like(o_ref)

    g = jax.nn.gelu(_dot(u_ref[...], xt_ref[...]))
    n_i = te // PEER_KEYS
    pieces = []
    for ii in range(n_i):
        i0 = et * n_i + ii
        s = jnp.zeros((PEER_KEYS, g.shape[1]), F32)
        for h in range(PEER_HEADS):
            lim = lim0_ref[h, pl.ds(i0, 1), :]
            cc = c0_ref[h, pl.ds(i0, 1), :]
            s = s + jnp.where(rank1_ref[h] < lim, e1_ref[h] * cc, 0.0)
        pieces.append((s * g[ii * PEER_KEYS:(ii + 1) * PEER_KEYS]).astype(BF16))
    w = jnp.concatenate(pieces, axis=0) if n_i > 1 else pieces[0]
    o_ref[...] += _dot(vt_ref[...], w)


def _peer_main(u, v_t, x_t, lim0, c0, rank1, e1, tn=512, te=512):
    n_exp, d = u.shape
    n = x_t.shape[1]
    tab = pl.BlockSpec((PEER_HEADS, PEER_KEYS, tn), lambda i, j: (0, 0, i))
    return pl.pallas_call(
        functools.partial(_peer_main_body, te=te),
        grid=(n // tn, n_exp // te),
        in_specs=[pl.BlockSpec((te, d), lambda i, j: (j, 0)), pl.BlockSpec((d, te), lambda i, j: (0, j)),
                  pl.BlockSpec((d, tn), lambda i, j: (0, i)), tab, tab, tab, tab],
        out_specs=pl.BlockSpec((d, tn), lambda i, j: (0, i)),
        out_shape=jax.ShapeDtypeStruct((d, n), F32),
        compiler_params=_cparams(("parallel", "arbitrary"), vmem_mb=56),
        name="peer_experts",
    )(u, v_t, x_t, lim0, c0, rank1, e1)


def kernel(x, attn_norm_g, w_in, cmp_k_pe, cmp_k_w1, cmp_k_w2, cmp_v_pe, cmp_v_w1, cmp_v_w2, rel_bias_table,
           w_branch_nsa, w_branch_sb, w_out, ffn_norm_g, peer_w_q, peer_sub_keys, peer_u, peer_v, final_norm_g):
    h = x
    for l in range(attn_norm_g.shape[0]):
        h = _layer(h, attn_norm_g[l], w_in[l], cmp_k_pe[l], cmp_k_w1[l], cmp_k_w2[l], cmp_v_pe[l], cmp_v_w1[l],
                   cmp_v_w2[l], rel_bias_table, w_branch_nsa[l], w_branch_sb[l], w_out[l], ffn_norm_g[l],
                   peer_w_q[l], peer_sub_keys[l], peer_u[l], peer_v[l],
                   final_norm_g if l == attn_norm_g.shape[0] - 1 else None)
    return h


def _layer(h, attn_g, w_in, ck_pe, ck_w1, ck_w2, cv_pe, cv_w1, cv_w2, rel_table, w_br_nsa, w_br_sb, w_out,
           ffn_g, pq, psk, pu, pv, final_g):
    b, t, dm = h.shape
    n = b * t
    g, hg, d = NSA_GROUPS, NSA_HPG, HEAD_DIM
    nq = t // TQ
    x2 = h.reshape(n, dm)

    o_gbr = NSA_Q + 6 * NSA_KV
    n_gbr = 3 * NSA_HEADS
    o_qb = o_gbr + n_gbr
    o_ga = o_qb + 3 * SB_W
    tn_in = 768
    packed = w_in.shape[1]
    packed_pad = -(-packed // tn_in) * tn_in
    w_pack = jnp.concatenate([w_in[:, o_ga:], w_in[:, :o_gbr], w_in[:, o_qb:o_ga], w_in[:, o_gbr:o_qb],
                              jnp.zeros((dm, packed_pad - packed), w_in.dtype)], axis=1).astype(BF16)
    a = _rmsnorm(x2, attn_g, BF16)
    proj = _matmul(a, w_pack, 512, tn_in, BF16, name="in_proj")
    col_ga, col_gb = 0, dm
    col_qn = 2 * dm
    col_kv = col_qn + NSA_Q
    col_qb = col_kv + 6 * NSA_KV
    col_gbr = col_qb + 3 * SB_W

    scale = jnp.asarray(HEAD_DIM ** -0.5, BF16)

    def heads_kv(z):
        return z.reshape(b, t, g, d).transpose(0, 2, 1, 3)

    def tiles_kv_t(z):
        return z.reshape(b, nq, TQ, g, d).transpose(0, 3, 1, 4, 2)

    q_n = (proj[:, col_qn:col_qn + NSA_Q] * scale).reshape(b, nq, TQ, g, hg, d)
    q_n = q_n.transpose(0, 3, 1, 5, 4, 2).reshape(b, g, nq, d, hg * TQ)
    kv_cols = [proj[:, col_kv + j * NSA_KV:col_kv + (j + 1) * NSA_KV] for j in range(6)]
    kc_tok, vc_tok = heads_kv(kv_cols[0]), heads_kv(kv_cols[1])
    ks, vs_t = heads_kv(kv_cols[2]), tiles_kv_t(kv_cols[3])
    kw, vw_t = heads_kv(kv_cols[4]), tiles_kv_t(kv_cols[5])

    n_cmp = (t - CMP_LEN) // CMP_STRIDE + 1
    n_chunk = t // CMP_STRIDE
    ncp = -(-n_cmp // LANES) * LANES
    reps = CMP_LEN // CMP_STRIDE

    def blocks(tok):
        ch = tok.reshape(b, g, n_chunk, CMP_STRIDE * d)
        ch = jnp.pad(ch, ((0, 0), (0, 0), (0, ncp + reps - 1 - n_chunk), (0, 0)))
        blk = jnp.concatenate([ch[:, :, r:r + ncp] for r in range(reps)], axis=-1)
        return blk.reshape(b * g * ncp, CMP_LEN * d)

    kc_blk = _compress(blocks(kc_tok), ck_pe.reshape(1, CMP_LEN * d), ck_w1, ck_w2, ncp).reshape(b, g, ncp, d)
    vc_blk = _compress(blocks(vc_tok), cv_pe.reshape(1, CMP_LEN * d), cv_w1, cv_w2, ncp).reshape(b, g, ncp, d)

    r_i = jnp.arange(TQ, dtype=jnp.int32)
    q_pos = jnp.arange(nq, dtype=jnp.int32)[:, None, None] * TQ + r_i[None, None, :]
    blk_end = jnp.arange(ncp, dtype=jnp.int32)[None, :, None] * CMP_STRIDE + CMP_LEN - 1
    bias_c = _bias_expand(rel_table, _rel_bucket(q_pos - blk_end))
    n_bias = min(nq, -(-(REL_MAX_DIST + TQ - 1) // TQ) + 1)
    dist_t = jnp.arange(n_bias, dtype=jnp.int32)[:, None, None] * TQ + r_i[None, None, :] - r_i[None, :, None]
    bias_t = _bias_expand(rel_table, _rel_bucket(dist_t))

    n_sel = t // SEL_LEN
    nsp = min(LANES, -(-n_sel // 16) * 16)
    c_start = np.arange(ncp) * CMP_STRIDE
    s_start = np.arange(nsp) * SEL_LEN
    overlap_t = np.maximum(np.minimum(c_start[None, :] + CMP_LEN, s_start[:, None] + SEL_LEN)
                           - np.maximum(c_start[None, :], s_start[:, None]), 0).astype(np.float32) / CMP_LEN
    overlap_t[:, n_cmp:] = 0.0
    overlap_t[n_sel:, :] = 0.0
    o_c, sel = _cmp_attention(q_n, kc_blk, vc_blk.transpose(0, 1, 3, 2), bias_c, jnp.asarray(overlap_t, BF16),
                              n_cmp, min(SEL_TOPK, n_sel))
    key_blk = (np.arange(t) // SEL_LEN).reshape(nq, TQ, 1)
    expand_t = (key_blk == np.arange(LANES)[None, None, :]).astype(np.float32)
    o_s = _nsa_attention(q_n, ks, vs_t, bias_t, sel, jnp.asarray(expand_t, BF16))
    o_w = _nsa_attention(q_n, kw, vw_t, bias_t)

    def tokens_nsa(o):
        o = o.reshape(b, g, nq, d, hg, TQ).transpose(0, 2, 5, 1, 4, 3)
        return o.reshape(n, NSA_Q)

    def heads_sb(z):
        return z.reshape(b, t, SB_HEADS, d)

    q_b = heads_sb(proj[:, col_qb:col_qb + SB_W] * scale)
    k_b = heads_sb(proj[:, col_qb + SB_W:col_qb + 2 * SB_W])
    v_b = heads_sb(proj[:, col_qb + 2 * SB_W:col_qb + 3 * SB_W])
    q_pad = jnp.concatenate([q_b, jnp.zeros_like(q_b)], axis=-1).transpose(0, 2, 1, 3)
    kv_b = jnp.concatenate([k_b, v_b], axis=-1).transpose(0, 2, 1, 3)
    o_sb = _stick_breaking(q_pad, kv_b)[..., d:].transpose(0, 2, 1, 3).reshape(n, SB_W)

    merged = _merge(tokens_nsa(o_c), tokens_nsa(o_s), tokens_nsa(o_w), o_sb, proj, col_gbr, col_ga, col_gb,
                    w_br_nsa, w_br_sb)
    h1 = _matmul(merged, w_out.astype(BF16), 512, 1024, F32, res=x2, name="out_proj")

    xn_t = _rmsnorm(h1, ffn_g, BF16).T
    keys = psk.reshape(PEER_HEADS * 2, PEER_KEYS, -1)
    s_t = _peer_scores(pq.T.astype(BF16), xn_t, keys)
    lim0, c0, rank1, e1 = _peer_route(s_t)
    ffn_t = _peer_main(pu.astype(BF16), pv.T.astype(BF16), xn_t, lim0, c0, rank1, e1)
    ffn = ffn_t.T
    if final_g is None:
        return (h1 + ffn).reshape(b, t, dm)
    return _add_rmsnorm(h1, ffn, final_g).reshape(b, t, dm)
```

```python
import functools
import math

import jax
import jax.numpy as jnp
import numpy as np
from jax import lax
from jax.experimental import pallas as pl
from jax.experimental.pallas import tpu as pltpu

F32 = jnp.float32
BF16 = jnp.bfloat16

HEAD_DIM = 64
NSA_HEADS = 16
NSA_GROUPS = 4
NSA_HPG = NSA_HEADS // NSA_GROUPS
SB_HEADS = 16
CMP_LEN = 32
CMP_STRIDE = 16
SEL_LEN = 64
SEL_TOPK = 16
N_LOCAL_SEL = 2
SEL_FORCE_BONUS = 1e4
WINDOW = 512
REL_BUCKETS = 32
REL_MAX_DIST = 1024
PEER_HEADS = 8
PEER_KEYS = 128
PEER_TOPK = 16
EPS = 1e-6
NEG = -1e30

NSA_Q = NSA_HEADS * HEAD_DIM
NSA_KV = NSA_GROUPS * HEAD_DIM
SB_W = SB_HEADS * HEAD_DIM

LANES = 128
SUBLANES = 8
TQ = 128
NSA_TILES_PER_ITER = 4
SB_HEADS_PER_STEP = 8
SB_SKIP_LOG = -110.0

_NT = (((1,), (1,)), ((), ()))


def _cparams(sem, vmem_mb=48):
    return pltpu.CompilerParams(dimension_semantics=sem, vmem_limit_bytes=vmem_mb * 1024 * 1024)


def _dot(a, b):
    return jnp.dot(a, b, preferred_element_type=F32)


def _dot_nt(a, b):
    return lax.dot_general(a, b, _NT, preferred_element_type=F32)


def _split_bf16(x):
    hi = x.astype(BF16)
    lo = (x - hi.astype(F32)).astype(BF16)
    return hi, lo


def _fold_rows(x, op):
    r, l = x.shape
    return op(x.reshape(r // SUBLANES, SUBLANES, l), axis=0)


def _rmsnorm_body(x_ref, g_ref, o_ref):
    x = x_ref[...]
    y = x * lax.rsqrt(jnp.mean(x * x, axis=-1, keepdims=True) + EPS)
    o_ref[...] = (y * g_ref[...]).astype(o_ref.dtype)


def _rmsnorm(x, g, out_dtype, tm=512):
    n, d = x.shape
    return pl.pallas_call(
        _rmsnorm_body,
        grid=(n // tm,),
        in_specs=[pl.BlockSpec((tm, d), lambda i: (i, 0)), pl.BlockSpec((1, d), lambda i: (0, 0))],
        out_specs=pl.BlockSpec((tm, d), lambda i: (i, 0)),
        out_shape=jax.ShapeDtypeStruct((n, d), out_dtype),
        compiler_params=_cparams(("parallel",)),
        name="rmsnorm",
    )(x, g.reshape(1, d))


def _add_rmsnorm_body(x_ref, y_ref, g_ref, o_ref):
    x = x_ref[...] + y_ref[...]
    y = x * lax.rsqrt(jnp.mean(x * x, axis=-1, keepdims=True) + EPS)
    o_ref[...] = (y * g_ref[...]).astype(o_ref.dtype)


def _add_rmsnorm(x, y, g, tm=512):
    n, d = x.shape
    return pl.pallas_call(
        _add_rmsnorm_body,
        grid=(n // tm,),
        in_specs=[pl.BlockSpec((tm, d), lambda i: (i, 0)), pl.BlockSpec((tm, d), lambda i: (i, 0)),
                  pl.BlockSpec((1, d), lambda i: (0, 0))],
        out_specs=pl.BlockSpec((tm, d), lambda i: (i, 0)),
        out_shape=jax.ShapeDtypeStruct((n, d), F32),
        compiler_params=_cparams(("parallel",)),
        name="add_rmsnorm",
    )(x, y, g.reshape(1, d))


def _mm_body(a_ref, w_ref, o_ref):
    o_ref[...] = _dot(a_ref[...], w_ref[...]).astype(o_ref.dtype)


def _mm_res_body(a_ref, w_ref, r_ref, o_ref):
    o_ref[...] = (r_ref[...] + _dot(a_ref[...], w_ref[...])).astype(o_ref.dtype)


def _matmul(a, w, tm, tn, out_dtype, res=None, name="matmul"):
    m, k = a.shape
    nc = w.shape[1]
    in_specs = [pl.BlockSpec((tm, k), lambda j, i: (i, 0)), pl.BlockSpec((k, tn), lambda j, i: (0, j))]
    args = [a, w]
    body = _mm_body
    if res is not None:
        in_specs.append(pl.BlockSpec((tm, tn), lambda j, i: (i, j)))
        args.append(res)
        body = _mm_res_body
    return pl.pallas_call(
        body,
        grid=(nc // tn, m // tm),
        in_specs=in_specs,
        out_specs=pl.BlockSpec((tm, tn), lambda j, i: (i, j)),
        out_shape=jax.ShapeDtypeStruct((m, nc), out_dtype),
        compiler_params=_cparams(("parallel", "parallel")),
        name=name,
    )(*args)


def _rel_bucket(dist):
    dist = jnp.maximum(dist, 0)
    n_exact = REL_BUCKETS // 2
    d_f = jnp.maximum(dist, 1).astype(jnp.float32)
    large = n_exact + (jnp.log(d_f / n_exact) / math.log(REL_MAX_DIST / n_exact)
                       * (REL_BUCKETS - n_exact)).astype(jnp.int32)
    large = jnp.minimum(large, REL_BUCKETS - 1)
    return jnp.where(dist < n_exact, dist, large)


def _bias_expand_body(tab_ref, bkt_ref, o_ref):
    h = pl.program_id(1)
    bkt = bkt_ref[0]
    acc = jnp.full(bkt.shape, tab_ref[0, h], F32)
    for k in range(1, REL_BUCKETS):
        acc = jnp.where(bkt == k, tab_ref[k, h], acc)
    o_ref[0, 0] = acc


def _bias_expand(table, bucket):
    r, rows, _ = bucket.shape
    return pl.pallas_call(
        _bias_expand_body,
        grid=(r, NSA_HEADS),
        in_specs=[pl.BlockSpec(memory_space=pltpu.SMEM), pl.BlockSpec((1, rows, TQ), lambda i, h: (i, 0, 0))],
        out_specs=pl.BlockSpec((1, 1, rows, TQ), lambda i, h: (h // NSA_HPG, i, 0, h % NSA_HPG)),
        out_shape=jax.ShapeDtypeStruct((NSA_GROUPS, r, rows, NSA_HPG * TQ), F32),
        compiler_params=_cparams(("parallel", "parallel")),
        name="bias_expand",
    )(table, bucket)


def _top_rows(x, k):
    r, l = x.shape
    ridx = lax.broadcasted_iota(jnp.int32, (r, l), 0).astype(F32)
    kidx = lax.broadcasted_iota(jnp.int32, (k, l), 0)
    rank = jnp.full((r, l), float(k), F32)
    vals = jnp.zeros((k, l), F32)
    cur = x
    for j in range(k):
        m = jnp.max(cur, axis=0, keepdims=True)
        idx = jnp.min(jnp.where(cur == m, ridx, float(r)), axis=0, keepdims=True)
        pick = ridx == idx
        rank = jnp.where(pick, float(j), rank)
        cur = jnp.where(pick, -jnp.inf, cur)
        vals = jnp.where(kidx == j, m, vals)
    return vals, rank


def _compress_body(blk_ref, pe_ref, w1_ref, w2_ref, o_ref):
    blk = (blk_ref[...].astype(F32) + pe_ref[...]).astype(BF16)
    h = jax.nn.gelu(_dot(blk, w1_ref[...]))
    o_ref[...] = _dot(h.astype(BF16), w2_ref[...]).astype(o_ref.dtype)


def _compress(blk, pe_flat, w1, w2, tr):
    r, kd = blk.shape
    hid = w1.shape[1]
    d = w2.shape[1]
    return pl.pallas_call(
        _compress_body,
        grid=(r // tr,),
        in_specs=[pl.BlockSpec((tr, kd), lambda i: (i, 0)), pl.BlockSpec((1, kd), lambda i: (0, 0)),
                  pl.BlockSpec((kd, hid), lambda i: (0, 0)), pl.BlockSpec((hid, d), lambda i: (0, 0))],
        out_specs=pl.BlockSpec((tr, d), lambda i: (i, 0)),
        out_shape=jax.ShapeDtypeStruct((r, d), BF16),
        compiler_params=_cparams(("parallel",)),
        name="compress_mlp",
    )(blk, pe_flat, w1.astype(BF16), w2.astype(BF16))


def _cmp_attn_body(q_ref, kc_ref, vct_ref, bias_ref, ovt_ref, o_ref, sel_ref, *, n_cmp, n_top):
    i = pl.program_id(2)
    q = q_ref[0, 0, 0]
    kc = kc_ref[0, 0]
    ncp, w = kc.shape[0], q.shape[1]
    blk = lax.broadcasted_iota(jnp.int32, (ncp, w), 0)
    t = i * TQ + jnp.bitwise_and(lax.broadcasted_iota(jnp.int32, (ncp, w), 1), TQ - 1)
    valid = ((t - CMP_STRIDE * blk - (CMP_LEN - 1)) >= 0) & (blk < n_cmp)
    s = jnp.where(valid, _dot(kc, q) + bias_ref[0, 0], NEG)
    m = jnp.max(s, axis=0, keepdims=True)
    e = jnp.where(valid, jnp.exp(s - m), 0.0)
    l = jnp.sum(e, axis=0, keepdims=True)
    p = e / jnp.maximum(l, 1e-30)
    o_ref[0, 0, 0] = _dot(vct_ref[0, 0], p.astype(BF16))
    psum = p[:, :TQ]
    for h in range(1, NSA_HPG):
        psum = psum + p[:, h * TQ:(h + 1) * TQ]
    hi, lo = _split_bf16(psum)
    ovt = ovt_ref[...]
    imp = _dot(ovt, hi) + _dot(ovt, lo)
    nsp = imp.shape[0]
    j = lax.broadcasted_iota(jnp.int32, (nsp, TQ), 0)
    cur = jnp.right_shift(i * TQ + lax.broadcasted_iota(jnp.int32, (nsp, TQ), 1), int(math.log2(SEL_LEN)))
    gap = cur - j
    forced = (j == 0) | ((gap >= 0) & (gap < N_LOCAL_SEL))
    allowed = j <= cur
    score = jnp.where(allowed, imp + SEL_FORCE_BONUS * forced.astype(F32), -jnp.inf)
    _, rank = _top_rows(score, n_top)
    sel = jnp.where((rank < float(n_top)) & allowed, 1.0, 0.0)
    if nsp < LANES:
        sel = jnp.concatenate([sel, jnp.zeros((LANES - nsp, TQ), F32)], axis=0)
    sel_ref[0, 0, 0] = sel.astype(sel_ref.dtype)


def _cmp_attention(q, kc, vc_t, bias_c, overlap_t, n_cmp, n_top):
    b, g, nq, d, w = q.shape
    ncp = kc.shape[2]
    return pl.pallas_call(
        functools.partial(_cmp_attn_body, n_cmp=n_cmp, n_top=n_top),
        grid=(b, g, nq),
        in_specs=[
            pl.BlockSpec((1, 1, 1, d, w), lambda bi, gi, i: (bi, gi, i, 0, 0)),
            pl.BlockSpec((1, 1, ncp, d), lambda bi, gi, i: (bi, gi, 0, 0)),
            pl.BlockSpec((1, 1, d, ncp), lambda bi, gi, i: (bi, gi, 0, 0)),
            pl.BlockSpec((1, 1, ncp, w), lambda bi, gi, i: (gi, i, 0, 0)),
            pl.BlockSpec(overlap_t.shape, lambda bi, gi, i: (0, 0)),
        ],
        out_specs=[
            pl.BlockSpec((1, 1, 1, d, w), lambda bi, gi, i: (bi, gi, i, 0, 0)),
            pl.BlockSpec((1, 1, 1, LANES, TQ), lambda bi, gi, i: (bi, gi, i, 0, 0)),
        ],
        out_shape=[jax.ShapeDtypeStruct((b, g, nq, d, w), F32), jax.ShapeDtypeStruct((b, g, nq, LANES, TQ), BF16)],
        compiler_params=_cparams(("parallel", "parallel", "parallel")),
        name="cmp_attention",
    )(q, kc, vc_t, bias_c, overlap_t)


def _nsa_attn_body(q_ref, k_ref, vt_ref, bias_ref, *rest, selected, n_bias):
    if selected:
        sel_ref, exp_ref, o_ref, s_ref = rest
        sel4 = jnp.concatenate([sel_ref[0, 0, 0]] * NSA_HPG, axis=1)
    else:
        o_ref, s_ref = rest
    i = pl.program_id(2)
    q = q_ref[0, 0, 0]
    d, w = q.shape
    nq = vt_ref.shape[2]
    key = lax.broadcasted_iota(jnp.int32, (TQ, w), 0)
    qk = jnp.bitwise_and(lax.broadcasted_iota(jnp.int32, (TQ, w), 1), TQ - 1) - key

    def scores(kt, slot):
        kt_ld = jnp.clip(kt, 0, nq - 1)
        ks = k_ref[0, 0, pl.ds(pl.multiple_of(kt_ld * TQ, TQ), TQ), :]
        delta = i - kt
        rel = delta * TQ + qk
        if selected:
            ok = (_dot(exp_ref[kt_ld], sel4) > 0.5) & (rel >= 0)
        else:
            ok = (rel >= 0) & (rel < WINDOW)
        ok = ok & ((kt >= 0) & (kt <= i))
        s = jnp.where(ok, _dot(ks, q) + bias_ref[0, jnp.clip(delta, 0, n_bias - 1)], NEG)
        s_ref[slot] = s
        return _fold_rows(s, jnp.max)

    def weights(kt, slot, m):
        p = jnp.exp(s_ref[slot] - m)
        return _fold_rows(p, jnp.sum), _dot(vt_ref[0, 0, jnp.clip(kt, 0, nq - 1)], p.astype(BF16))

    mx = jnp.full((SUBLANES, w), NEG, F32)
    l8 = jnp.zeros((SUBLANES, w), F32)
    acc = jnp.zeros((d, w), F32)
    if selected:
        n_it = (i + NSA_TILES_PER_ITER) // NSA_TILES_PER_ITER

        def pass1(it, mx):
            for u in range(NSA_TILES_PER_ITER):
                kt = it * NSA_TILES_PER_ITER + u
                mx = jnp.maximum(mx, scores(kt, kt))
            return mx

        m = jnp.max(lax.fori_loop(0, n_it, pass1, mx), axis=0, keepdims=True)

        def pass2(it, carry):
            l8, acc = carry
            for u in range(NSA_TILES_PER_ITER):
                kt = it * NSA_TILES_PER_ITER + u
                dl, da = weights(kt, kt, m)
                l8, acc = l8 + dl, acc + da
            return l8, acc

        l8, acc = lax.fori_loop(0, n_it, pass2, (l8, acc))
    else:
        n_win = s_ref.shape[0]
        for u in range(n_win):
            mx = jnp.maximum(mx, scores(i - (n_win - 1) + u, u))
        m = jnp.max(mx, axis=0, keepdims=True)
        for u in range(n_win):
            dl, da = weights(i - (n_win - 1) + u, u, m)
            l8, acc = l8 + dl, acc + da
    o_ref[0, 0, 0] = acc / jnp.sum(l8, axis=0, keepdims=True)


def _nsa_attention(q, k, v_t, bias_t, sel=None, expand_t=None):
    b, g, nq, d, w = q.shape
    t = k.shape[2]
    n_bias = bias_t.shape[1]
    selected = sel is not None
    in_specs = [
        pl.BlockSpec((1, 1, 1, d, w), lambda bi, gi, i: (bi, gi, i, 0, 0)),
        pl.BlockSpec((1, 1, t, d), lambda bi, gi, i: (bi, gi, 0, 0)),
        pl.BlockSpec((1, 1, nq, d, TQ), lambda bi, gi, i: (bi, gi, 0, 0, 0)),
        pl.BlockSpec((1, n_bias, TQ, w), lambda bi, gi, i: (gi, 0, 0, 0)),
    ]
    args = [q, k, v_t, bias_t]
    if selected:
        in_specs += [pl.BlockSpec((1, 1, 1, LANES, TQ), lambda bi, gi, i: (bi, gi, i, 0, 0)),
                     pl.BlockSpec(expand_t.shape, lambda bi, gi, i: (0, 0, 0))]
        args += [sel, expand_t]
    return pl.pallas_call(
        functools.partial(_nsa_attn_body, selected=selected, n_bias=n_bias),
        grid=(b, g, nq),
        in_specs=in_specs,
        out_specs=pl.BlockSpec((1, 1, 1, d, w), lambda bi, gi, i: (bi, gi, i, 0, 0)),
        out_shape=jax.ShapeDtypeStruct((b, g, nq, d, w), F32),
        scratch_shapes=[pltpu.VMEM((nq + NSA_TILES_PER_ITER - 1 if selected else WINDOW // TQ + 1, TQ, w), F32)],
        compiler_params=_cparams(("parallel", "parallel", "parallel")),
        name="sel_attention" if selected else "win_attention",
    )(*args)


def _sb_body(q_ref, kv_ref, uo_ref, o_ref, c_ref):
    i = pl.program_id(2)
    uo = uo_ref[...]
    hb = q_ref.shape[1]
    row = lax.broadcasted_iota(jnp.int32, (TQ, TQ), 0)
    col = lax.broadcasted_iota(jnp.int32, (TQ, TQ), 1)
    rc = row - col
    c_ref[...] = jnp.zeros_like(c_ref)
    o_ref[...] = jnp.zeros_like(o_ref)

    def cond(st):
        kt, cmax = st
        return (kt >= 0) & (cmax > SB_SKIP_LOG)

    def body(st):
        kt, _ = st
        start = pl.multiple_of(kt * TQ, TQ)
        mask = ((i - kt) * TQ + rc) > 0
        heads = range(hb)
        kvs = [kv_ref[0, h, pl.ds(start, TQ), :] for h in heads]
        zs = [_dot_nt(q_ref[0, h], kvs[h]) for h in heads]
        lbs = [jnp.minimum(z, 0.0) - jnp.log(1.0 + jnp.exp(-jnp.abs(z))) for z in zs]
        parts = [_split_bf16(jnp.where(mask, lbs[h] - zs[h], 0.0)) for h in heads]
        sfxs = [_dot(hi, uo) + _dot(lo, uo) for hi, lo in parts]
        ws = [jnp.where(mask, jnp.exp(lbs[h] + sfxs[h][:, :TQ] + c_ref[h]), 0.0).astype(BF16) for h in heads]
        cm = None
        for h in heads:
            o_ref[0, h] += _dot(ws[h], kvs[h])
            c = c_ref[h] + sfxs[h][:, TQ:]
            c_ref[h] = c
            cm = c if cm is None else jnp.maximum(cm, c)
        return kt - 1, jnp.max(cm)

    lax.while_loop(cond, body, (i, jnp.float32(0.0)))


def _stick_breaking(q_pad, kv):
    b, h, t, d2 = q_pad.shape
    hb = SB_HEADS_PER_STEP
    tri = np.triu(np.ones((TQ, TQ), np.float32), 0).T - np.eye(TQ, dtype=np.float32)
    uo = jnp.asarray(np.concatenate([tri, np.ones((TQ, TQ), np.float32)], axis=1), BF16)
    return pl.pallas_call(
        _sb_body,
        grid=(b, h // hb, t // TQ),
        in_specs=[
            pl.BlockSpec((1, hb, TQ, d2), lambda bi, hi, i: (bi, hi, i, 0)),
            pl.BlockSpec((1, hb, t, d2), lambda bi, hi, i: (bi, hi, 0, 0)),
            pl.BlockSpec((TQ, 2 * TQ), lambda bi, hi, i: (0, 0)),
        ],
        out_specs=pl.BlockSpec((1, hb, TQ, d2), lambda bi, hi, i: (bi, hi, i, 0)),
        out_shape=jax.ShapeDtypeStruct((b, h, t, d2), F32),
        scratch_shapes=[pltpu.VMEM((hb, TQ, TQ), F32)],
        compiler_params=_cparams(("parallel", "parallel", "parallel")),
        name="stick_breaking",
    )(q_pad, kv, uo)


def _merge_body(oc_ref, os_ref, ow_ref, osb_ref, gbr_ref, ga_ref, gb_ref, e_ref, wn_ref, wsb_ref, o_ref):
    gate = jax.nn.sigmoid(gbr_ref[...].astype(F32))
    hi, lo = _split_bf16(gate)
    e = e_ref[...]
    gexp = _dot(hi, e) + _dot(lo, e)
    o_nsa = (gexp[:, :NSA_Q] * oc_ref[...] + gexp[:, NSA_Q:2 * NSA_Q] * os_ref[...]
             + gexp[:, 2 * NSA_Q:] * ow_ref[...])
    a = _dot(o_nsa.astype(BF16), wn_ref[...])
    bm = _dot(osb_ref[...].astype(BF16), wsb_ref[...])
    merged = jax.nn.sigmoid(ga_ref[...].astype(F32)) * a + jax.nn.sigmoid(gb_ref[...].astype(F32)) * bm
    o_ref[...] = merged.astype(o_ref.dtype)


def _merge(o_c, o_s, o_w, o_sb, proj, col_gbr, col_ga, col_gb, w_nsa, w_sb, tm=256):
    n = o_c.shape[0]
    dm = w_nsa.shape[1]
    e = np.zeros((LANES, 3 * NSA_Q), np.float32)
    for j in range(3 * NSA_HEADS):
        e[j, j * HEAD_DIM:(j + 1) * HEAD_DIM] = 1.0
    row = lambda i: (i, 0)
    full = lambda i: (0, 0)
    return pl.pallas_call(
        _merge_body,
        grid=(n // tm,),
        in_specs=[
            pl.BlockSpec((tm, NSA_Q), row), pl.BlockSpec((tm, NSA_Q), row), pl.BlockSpec((tm, NSA_Q), row),
            pl.BlockSpec((tm, SB_W), row),
            pl.BlockSpec((tm, LANES), lambda i: (i, col_gbr // LANES)),
            pl.BlockSpec((tm, dm), lambda i: (i, col_ga // dm)),
            pl.BlockSpec((tm, dm), lambda i: (i, col_gb // dm)),
            pl.BlockSpec((LANES, 3 * NSA_Q), full),
            pl.BlockSpec((NSA_Q, dm), full), pl.BlockSpec((SB_W, dm), full),
        ],
        out_specs=pl.BlockSpec((tm, dm), row),
        out_shape=jax.ShapeDtypeStruct((n, dm), BF16),
        compiler_params=_cparams(("parallel",)),
        name="gated_merge",
    )(o_c, o_s, o_w, o_sb, proj, proj, proj, jnp.asarray(e, BF16), w_nsa.astype(BF16), w_sb.astype(BF16))


def _peer_scores_body(wqt_ref, xt_ref, keys_ref, s_ref):
    qt = _dot(wqt_ref[...], xt_ref[...])
    c = keys_ref.shape[-1]
    for ch in range(keys_ref.shape[0]):
        qh, ql = _split_bf16(qt[ch * c:(ch + 1) * c])
        kh, kl = _split_bf16(keys_ref[ch])
        s_ref[ch * PEER_KEYS:(ch + 1) * PEER_KEYS, :] = _dot(kh, qh) + _dot(kh, ql) + _dot(kl, qh)


def _peer_scores(wq_t, x_t, keys, tn=512):
    rq, d = wq_t.shape
    n = x_t.shape[1]
    nch, nk, c = keys.shape
    return pl.pallas_call(
        _peer_scores_body,
        grid=(n // tn,),
        in_specs=[pl.BlockSpec((rq, d), lambda i: (0, 0)), pl.BlockSpec((d, tn), lambda i: (0, i)),
                  pl.BlockSpec((nch, nk, c), lambda i: (0, 0, 0))],
        out_specs=pl.BlockSpec((nch * nk, tn), lambda i: (0, i)),
        out_shape=jax.ShapeDtypeStruct((nch * nk, n), F32),
        compiler_params=_cparams(("parallel",)),
        name="peer_scores",
    )(wq_t, x_t, keys)


def _peer_route_body(s_ref, lim0_ref, c0_ref, rank1_ref, e1_ref):
    k = PEER_TOPK
    tn = s_ref.shape[-1]
    kidx = lax.broadcasted_iota(jnp.int32, (k, tn), 0)
    flat = lax.broadcasted_iota(jnp.int32, (k * k, tn), 0)
    reachable = (jnp.right_shift(flat, int(math.log2(k))) + 1) * (jnp.bitwise_and(flat, k - 1) + 1) <= k
    for h in range(PEER_HEADS):
        s0 = s_ref[(2 * h) * PEER_KEYS:(2 * h + 1) * PEER_KEYS, :]
        s1 = s_ref[(2 * h + 1) * PEER_KEYS:(2 * h + 2) * PEER_KEYS, :]
        a, rank0 = _top_rows(s0, k)
        b, rank1 = _top_rows(s1, k)
        cand = jnp.concatenate([a[r:r + 1] + b for r in range(k)], axis=0)
        cand = jnp.where(reachable, cand, -jnp.inf)
        best, crank = _top_rows(cand, k)
        chosen = jnp.where(crank < float(k), 1.0, 0.0)
        e = chosen * jnp.exp(jnp.minimum(cand - best[0:1], 0.0))
        z = jnp.sum(e, axis=0, keepdims=True)
        cnt = jnp.zeros((k, tn), F32)
        for r in range(k):
            n_r = jnp.sum(chosen[r * k:(r + 1) * k], axis=0, keepdims=True)
            cnt = jnp.where(kidx == r, n_r, cnt)
        lim0 = jnp.zeros_like(s0)
        for r in range(k):
            lim0 = jnp.where(rank0 == float(r), cnt[r:r + 1], lim0)
        lim0_ref[h] = lim0
        c0_ref[h] = jnp.exp(s0 - a[0:1]) / z
        rank1_ref[h] = rank1.astype(rank1_ref.dtype)
        e1_ref[h] = jnp.exp(s1 - b[0:1]).astype(e1_ref.dtype)


def _peer_route(s_t, tn=256):
    rows, n = s_t.shape
    shp = jax.ShapeDtypeStruct((PEER_HEADS, PEER_KEYS, n), F32)
    shp16 = jax.ShapeDtypeStruct((PEER_HEADS, PEER_KEYS, n), BF16)
    spec = pl.BlockSpec((PEER_HEADS, PEER_KEYS, tn), lambda i: (0, 0, i))
    return pl.pallas_call(
        _peer_route_body,
        grid=(n // tn,),
        in_specs=[pl.BlockSpec((rows, tn), lambda i: (0, i))],
        out_specs=[spec, spec, spec, spec],
        out_shape=[shp, shp, shp16, shp16],
        compiler_params=_cparams(("parallel",)),
        name="peer_route",
    )(s_t)


def _peer_main_body(u_ref, vt_ref, xt_ref, lim0_ref, c0_ref, rank1_ref, e1_ref, o_ref, act_ref, w_ref, *, te, ts):
    et = pl.program_id(1)

    @pl.when(et == 0)
    def _():
        o_ref[...] = jnp.zeros_like(o_ref)

    xt = xt_ref[...]
    tn = xt.shape[1]
    n_i = ts // PEER_KEYS
    for sb in range(te // ts):
        act_ref[sb] = _dot(u_ref[sb * ts:(sb + 1) * ts, :], xt)
    pk = 2 * SUBLANES
    grp = (PEER_KEYS // pk, pk, LANES)
    for sb in range(te // ts):
        for ii in range(n_i):
            r = sb * n_i + ii
            rows = slice(ii * PEER_KEYS, (ii + 1) * PEER_KEYS)
            for lc in range(tn // LANES):
                cols = slice(lc * LANES, (lc + 1) * LANES)
                s = jnp.zeros(grp, BF16)
                for h in range(PEER_HEADS):
                    lim = jnp.broadcast_to(lim0_ref[h, r:r + 1, cols], (pk, LANES)).astype(BF16)
                    cc = jnp.broadcast_to(c0_ref[h, r:r + 1, cols], (pk, LANES)).astype(BF16)
                    s = s + jnp.where(rank1_ref[h, :, cols].reshape(grp) < lim[None],
                                      e1_ref[h, :, cols].reshape(grp) * cc[None], jnp.zeros((), BF16))
                g = jax.nn.gelu(act_ref[sb, rows, cols]).astype(BF16)
                w_ref[sb, rows, cols] = s.reshape(PEER_KEYS, LANES) * g
        o_ref[...] += _dot(vt_ref[:, sb * ts:(sb + 1) * ts], w_ref[sb])


def _peer_main(u, v_t, x_t, lim0, c0, rank1, e1, tn=512, te=1024, ts=256):
    n_exp, d = u.shape
    n = x_t.shape[1]
    once = pl.Buffered(1)
    tab = pl.BlockSpec((PEER_HEADS, PEER_KEYS, tn), lambda i, j: (0, 0, i), pipeline_mode=once)
    tab0 = pl.BlockSpec((PEER_HEADS, te // PEER_KEYS, tn), lambda i, j: (0, j, i))
    return pl.pallas_call(
        functools.partial(_peer_main_body, te=te, ts=ts),
        grid=(n // tn, n_exp // te),
        in_specs=[pl.BlockSpec((te, d), lambda i, j: (j, 0)), pl.BlockSpec((d, te), lambda i, j: (0, j)),
                  pl.BlockSpec((d, tn), lambda i, j: (0, i), pipeline_mode=once), tab0, tab0, tab, tab],
        out_specs=pl.BlockSpec((d, tn), lambda i, j: (0, i)),
        out_shape=jax.ShapeDtypeStruct((d, n), F32),
        scratch_shapes=[pltpu.VMEM((te // ts, ts, tn), F32), pltpu.VMEM((te // ts, ts, tn), BF16)],
        compiler_params=_cparams(("parallel", "arbitrary"), vmem_mb=56),
        name="peer_experts",
    )(u, v_t, x_t, lim0, c0, rank1, e1)


def kernel(x, attn_norm_g, w_in, cmp_k_pe, cmp_k_w1, cmp_k_w2, cmp_v_pe, cmp_v_w1, cmp_v_w2, rel_bias_table,
           w_branch_nsa, w_branch_sb, w_out, ffn_norm_g, peer_w_q, peer_sub_keys, peer_u, peer_v, final_norm_g):
    h = x
    for l in range(attn_norm_g.shape[0]):
        h = _layer(h, attn_norm_g[l], w_in[l], cmp_k_pe[l], cmp_k_w1[l], cmp_k_w2[l], cmp_v_pe[l], cmp_v_w1[l],
                   cmp_v_w2[l], rel_bias_table, w_branch_nsa[l], w_branch_sb[l], w_out[l], ffn_norm_g[l],
                   peer_w_q[l], peer_sub_keys[l], peer_u[l], peer_v[l],
                   final_norm_g if l == attn_norm_g.shape[0] - 1 else None)
    return h


def _layer(h, attn_g, w_in, ck_pe, ck_w1, ck_w2, cv_pe, cv_w1, cv_w2, rel_table, w_br_nsa, w_br_sb, w_out,
           ffn_g, pq, psk, pu, pv, final_g):
    b, t, dm = h.shape
    n = b * t
    g, hg, d = NSA_GROUPS, NSA_HPG, HEAD_DIM
    nq = t // TQ
    x2 = h.reshape(n, dm)

    o_gbr = NSA_Q + 6 * NSA_KV
    n_gbr = 3 * NSA_HEADS
    o_qb = o_gbr + n_gbr
    o_ga = o_qb + 3 * SB_W
    tn_in = 768
    packed = w_in.shape[1]
    packed_pad = -(-packed // tn_in) * tn_in
    w_pack = jnp.concatenate([w_in[:, o_ga:], w_in[:, :o_gbr], w_in[:, o_qb:o_ga], w_in[:, o_gbr:o_qb],
                              jnp.zeros((dm, packed_pad - packed), w_in.dtype)], axis=1).astype(BF16)
    a = _rmsnorm(x2, attn_g, BF16)
    proj = _matmul(a, w_pack, 512, tn_in, BF16, name="in_proj")
    col_ga, col_gb = 0, dm
    col_qn = 2 * dm
    col_kv = col_qn + NSA_Q
    col_qb = col_kv + 6 * NSA_KV
    col_gbr = col_qb + 3 * SB_W

    scale = jnp.asarray(HEAD_DIM ** -0.5, BF16)

    def heads_kv(z):
        return z.reshape(b, t, g, d).transpose(0, 2, 1, 3)

    def tiles_kv_t(z):
        return z.reshape(b, nq, TQ, g, d).transpose(0, 3, 1, 4, 2)

    q_n = (proj[:, col_qn:col_qn + NSA_Q] * scale).reshape(b, nq, TQ, g, hg, d)
    q_n = q_n.transpose(0, 3, 1, 5, 4, 2).reshape(b, g, nq, d, hg * TQ)
    kv_cols = [proj[:, col_kv + j * NSA_KV:col_kv + (j + 1) * NSA_KV] for j in range(6)]
    kc_tok, vc_tok = heads_kv(kv_cols[0]), heads_kv(kv_cols[1])
    ks, vs_t = heads_kv(kv_cols[2]), tiles_kv_t(kv_cols[3])
    kw, vw_t = heads_kv(kv_cols[4]), tiles_kv_t(kv_cols[5])

    n_cmp = (t - CMP_LEN) // CMP_STRIDE + 1
    n_chunk = t // CMP_STRIDE
    ncp = -(-n_cmp // LANES) * LANES
    reps = CMP_LEN // CMP_STRIDE

    def blocks(tok):
        ch = tok.reshape(b, g, n_chunk, CMP_STRIDE * d)
        ch = jnp.pad(ch, ((0, 0), (0, 0), (0, ncp + reps - 1 - n_chunk), (0, 0)))
        blk = jnp.concatenate([ch[:, :, r:r + ncp] for r in range(reps)], axis=-1)
        return blk.reshape(b * g * ncp, CMP_LEN * d)

    kc_blk = _compress(blocks(kc_tok), ck_pe.reshape(1, CMP_LEN * d), ck_w1, ck_w2, ncp).reshape(b, g, ncp, d)
    vc_blk = _compress(blocks(vc_tok), cv_pe.reshape(1, CMP_LEN * d), cv_w1, cv_w2, ncp).reshape(b, g, ncp, d)

    r_i = jnp.arange(TQ, dtype=jnp.int32)
    q_pos = jnp.arange(nq, dtype=jnp.int32)[:, None, None] * TQ + r_i[None, None, :]
    blk_end = jnp.arange(ncp, dtype=jnp.int32)[None, :, None] * CMP_STRIDE + CMP_LEN - 1
    bias_c = _bias_expand(rel_table, _rel_bucket(q_pos - blk_end))
    n_bias = min(nq, -(-(REL_MAX_DIST + TQ - 1) // TQ) + 1)
    dist_t = jnp.arange(n_bias, dtype=jnp.int32)[:, None, None] * TQ + r_i[None, None, :] - r_i[None, :, None]
    bias_t = _bias_expand(rel_table, _rel_bucket(dist_t))

    n_sel = t // SEL_LEN
    nsp = min(LANES, -(-n_sel // 16) * 16)
    c_start = np.arange(ncp) * CMP_STRIDE
    s_start = np.arange(nsp) * SEL_LEN
    overlap_t = np.maximum(np.minimum(c_start[None, :] + CMP_LEN, s_start[:, None] + SEL_LEN)
                           - np.maximum(c_start[None, :], s_start[:, None]), 0).astype(np.float32) / CMP_LEN
    overlap_t[:, n_cmp:] = 0.0
    overlap_t[n_sel:, :] = 0.0
    o_c, sel = _cmp_attention(q_n, kc_blk, vc_blk.transpose(0, 1, 3, 2), bias_c, jnp.asarray(overlap_t, BF16),
                              n_cmp, min(SEL_TOPK, n_sel))
    key_blk = (np.arange(t) // SEL_LEN).reshape(nq, TQ, 1)
    expand_t = (key_blk == np.arange(LANES)[None, None, :]).astype(np.float32)
    o_s = _nsa_attention(q_n, ks, vs_t, bias_t, sel, jnp.asarray(expand_t, BF16))
    o_w = _nsa_attention(q_n, kw, vw_t, bias_t)

    def tokens_nsa(o):
        o = o.reshape(b, g, nq, d, hg, TQ).transpose(0, 2, 5, 1, 4, 3)
        return o.reshape(n, NSA_Q)

    def heads_sb(z):
        return z.reshape(b, t, SB_HEADS, d)

    q_b = heads_sb(proj[:, col_qb:col_qb + SB_W] * scale)
    k_b = heads_sb(proj[:, col_qb + SB_W:col_qb + 2 * SB_W])
    v_b = heads_sb(proj[:, col_qb + 2 * SB_W:col_qb + 3 * SB_W])
    q_pad = jnp.concatenate([q_b, jnp.zeros_like(q_b)], axis=-1).transpose(0, 2, 1, 3)
    kv_b = jnp.concatenate([k_b, v_b], axis=-1).transpose(0, 2, 1, 3)
    o_sb = _stick_breaking(q_pad, kv_b)[..., d:].transpose(0, 2, 1, 3).reshape(n, SB_W)

    merged = _merge(tokens_nsa(o_c), tokens_nsa(o_s), tokens_nsa(o_w), o_sb, proj, col_gbr, col_ga, col_gb,
                    w_br_nsa, w_br_sb)
    h1 = _matmul(merged, w_out.astype(BF16), 512, 1024, F32, res=x2, name="out_proj")

    xn_t = _rmsnorm(h1, ffn_g, BF16).T
    keys = psk.reshape(PEER_HEADS * 2, PEER_KEYS, -1)
    s_t = _peer_scores(pq.T.astype(BF16), xn_t, keys)
    lim0, c0, rank1, e1 = _peer_route(s_t)
    ffn_t = _peer_main(pu.astype(BF16), pv.T.astype(BF16), xn_t, lim0, c0, rank1, e1)
    ffn = ffn_t.T
    if final_g is None:
        return (h1 + ffn).reshape(b, t, dm)
    return _add_rmsnorm(h1, ffn, final_g).reshape(b, t, dm)
```

```python
import functools
import math

import jax
import jax.numpy as jnp
import numpy as np
from jax import lax
from jax.experimental import pallas as pl
from jax.experimental.pallas import tpu as pltpu

F32 = jnp.float32
BF16 = jnp.bfloat16

HEAD_DIM = 64
NSA_HEADS = 16
NSA_GROUPS = 4
NSA_HPG = NSA_HEADS // NSA_GROUPS
SB_HEADS = 16
CMP_LEN = 32
CMP_STRIDE = 16
SEL_LEN = 64
SEL_TOPK = 16
N_LOCAL_SEL = 2
SEL_FORCE_BONUS = 1e4
WINDOW = 512
REL_BUCKETS = 32
REL_MAX_DIST = 1024
PEER_HEADS = 8
PEER_KEYS = 128
PEER_TOPK = 16
EPS = 1e-6
NEG = -1e30

NSA_Q = NSA_HEADS * HEAD_DIM
NSA_KV = NSA_GROUPS * HEAD_DIM
SB_W = SB_HEADS * HEAD_DIM

LANES = 128
SUBLANES = 8
TQ = 128
NSA_TILES_PER_ITER = 4
SB_HEADS_PER_STEP = 8
SB_SKIP_LOG = -110.0

_NT = (((1,), (1,)), ((), ()))


def _cparams(sem, vmem_mb=48):
    return pltpu.CompilerParams(dimension_semantics=sem, vmem_limit_bytes=vmem_mb * 1024 * 1024)


def _dot(a, b):
    return jnp.dot(a, b, preferred_element_type=F32)


def _dot_nt(a, b):
    return lax.dot_general(a, b, _NT, preferred_element_type=F32)


def _split_bf16(x):
    hi = x.astype(BF16)
    lo = (x - hi.astype(F32)).astype(BF16)
    return hi, lo


def _fold_rows(x, op):
    r, l = x.shape
    return op(x.reshape(r // SUBLANES, SUBLANES, l), axis=0)


def _rmsnorm_body(x_ref, g_ref, o_ref):
    x = x_ref[...]
    y = x * lax.rsqrt(jnp.mean(x * x, axis=-1, keepdims=True) + EPS)
    o_ref[...] = (y * g_ref[...]).astype(o_ref.dtype)


def _rmsnorm(x, g, out_dtype, tm=512):
    n, d = x.shape
    return pl.pallas_call(
        _rmsnorm_body,
        grid=(n // tm,),
        in_specs=[pl.BlockSpec((tm, d), lambda i: (i, 0)), pl.BlockSpec((1, d), lambda i: (0, 0))],
        out_specs=pl.BlockSpec((tm, d), lambda i: (i, 0)),
        out_shape=jax.ShapeDtypeStruct((n, d), out_dtype),
        compiler_params=_cparams(("parallel",)),
        name="rmsnorm",
    )(x, g.reshape(1, d))


def _add_rmsnorm_body(x_ref, y_ref, g_ref, o_ref):
    x = x_ref[...] + y_ref[...]
    y = x * lax.rsqrt(jnp.mean(x * x, axis=-1, keepdims=True) + EPS)
    o_ref[...] = (y * g_ref[...]).astype(o_ref.dtype)


def _add_rmsnorm(x, y, g, tm=512):
    n, d = x.shape
    return pl.pallas_call(
        _add_rmsnorm_body,
        grid=(n // tm,),
        in_specs=[pl.BlockSpec((tm, d), lambda i: (i, 0)), pl.BlockSpec((tm, d), lambda i: (i, 0)),
                  pl.BlockSpec((1, d), lambda i: (0, 0))],
        out_specs=pl.BlockSpec((tm, d), lambda i: (i, 0)),
        out_shape=jax.ShapeDtypeStruct((n, d), F32),
        compiler_params=_cparams(("parallel",)),
        name="add_rmsnorm",
    )(x, y, g.reshape(1, d))


def _mm_body(a_ref, w_ref, o_ref):
    o_ref[...] = _dot(a_ref[...], w_ref[...]).astype(o_ref.dtype)


def _mm_res_body(a_ref, w_ref, r_ref, o_ref):
    o_ref[...] = (r_ref[...] + _dot(a_ref[...], w_ref[...])).astype(o_ref.dtype)


def _matmul(a, w, tm, tn, out_dtype, res=None, name="matmul"):
    m, k = a.shape
    nc = w.shape[1]
    in_specs = [pl.BlockSpec((tm, k), lambda j, i: (i, 0)), pl.BlockSpec((k, tn), lambda j, i: (0, j))]
    args = [a, w]
    body = _mm_body
    if res is not None:
        in_specs.append(pl.BlockSpec((tm, tn), lambda j, i: (i, j)))
        args.append(res)
        body = _mm_res_body
    return pl.pallas_call(
        body,
        grid=(nc // tn, m // tm),
        in_specs=in_specs,
        out_specs=pl.BlockSpec((tm, tn), lambda j, i: (i, j)),
        out_shape=jax.ShapeDtypeStruct((m, nc), out_dtype),
        compiler_params=_cparams(("parallel", "parallel")),
        name=name,
    )(*args)


def _rel_bucket(dist):
    dist = jnp.maximum(dist, 0)
    n_exact = REL_BUCKETS // 2
    d_f = jnp.maximum(dist, 1).astype(jnp.float32)
    large = n_exact + (jnp.log(d_f / n_exact) / math.log(REL_MAX_DIST / n_exact)
                       * (REL_BUCKETS - n_exact)).astype(jnp.int32)
    large = jnp.minimum(large, REL_BUCKETS - 1)
    return jnp.bitwise_and(jnp.where(dist < n_exact, dist, large), REL_BUCKETS - 1)


def _bias_expand_body(tab_ref, bkt_ref, o_ref):
    h = pl.program_id(1)
    bkt = bkt_ref[0]
    acc = jnp.full(bkt.shape, tab_ref[0, h], F32)
    for k in range(1, REL_BUCKETS):
        acc = jnp.where(bkt >= k, tab_ref[k, h], acc)
    o_ref[0, 0] = acc


def _bias_expand(table, bucket):
    r, rows, _ = bucket.shape
    return pl.pallas_call(
        _bias_expand_body,
        grid=(r, NSA_HEADS),
        in_specs=[pl.BlockSpec(memory_space=pltpu.SMEM), pl.BlockSpec((1, rows, TQ), lambda i, h: (i, 0, 0))],
        out_specs=pl.BlockSpec((1, 1, rows, TQ), lambda i, h: (h // NSA_HPG, i, 0, h % NSA_HPG)),
        out_shape=jax.ShapeDtypeStruct((NSA_GROUPS, r, rows, NSA_HPG * TQ), F32),
        compiler_params=_cparams(("parallel", "parallel")),
        name="bias_expand",
    )(table, bucket)


def _top_rows(x, k):
    r, l = x.shape
    ridx = lax.broadcasted_iota(jnp.int32, (r, l), 0).astype(F32)
    kidx = lax.broadcasted_iota(jnp.int32, (k, l), 0)
    rank = jnp.full((r, l), float(k), F32)
    vals = jnp.zeros((k, l), F32)
    cur = x
    for j in range(k):
        m = jnp.max(cur, axis=0, keepdims=True)
        idx = jnp.min(jnp.where(cur == m, ridx, float(r)), axis=0, keepdims=True)
        pick = ridx == idx
        rank = jnp.where(pick, float(j), rank)
        cur = jnp.where(pick, -jnp.inf, cur)
        vals = jnp.where(kidx == j, m, vals)
    return vals, rank


def _compress_body(blk_ref, pe_ref, w1_ref, w2_ref, o_ref):
    blk = (blk_ref[...].astype(F32) + pe_ref[...]).astype(BF16)
    h = jax.nn.gelu(_dot(blk, w1_ref[...]))
    o_ref[...] = _dot(h.astype(BF16), w2_ref[...]).astype(o_ref.dtype)


def _compress(blk, pe_flat, w1, w2, tr):
    r, kd = blk.shape
    hid = w1.shape[1]
    d = w2.shape[1]
    return pl.pallas_call(
        _compress_body,
        grid=(r // tr,),
        in_specs=[pl.BlockSpec((tr, kd), lambda i: (i, 0)), pl.BlockSpec((1, kd), lambda i: (0, 0)),
                  pl.BlockSpec((kd, hid), lambda i: (0, 0)), pl.BlockSpec((hid, d), lambda i: (0, 0))],
        out_specs=pl.BlockSpec((tr, d), lambda i: (i, 0)),
        out_shape=jax.ShapeDtypeStruct((r, d), BF16),
        compiler_params=_cparams(("parallel",)),
        name="compress_mlp",
    )(blk, pe_flat, w1.astype(BF16), w2.astype(BF16))


def _heads_on_lanes(q_blk):
    qt = q_blk.astype(F32).T
    d = qt.shape[0] // NSA_HPG
    return jnp.concatenate([qt[h * d:(h + 1) * d] for h in range(NSA_HPG)], axis=1).astype(BF16)


def _heads_to_tokens(o):
    return jnp.concatenate([o[:, h * TQ:(h + 1) * TQ] for h in range(NSA_HPG)], axis=0).T


def _cmp_attn_body(q_ref, kc_ref, vct_ref, bias_ref, ovt_ref, o_ref, sel_ref, *, n_cmp, n_top):
    i = pl.program_id(2)
    nq = pl.num_programs(2)
    q = _heads_on_lanes(q_ref[0])
    kc = kc_ref[0, 0]
    ncp, w = kc.shape[0], q.shape[1]
    blk = lax.broadcasted_iota(jnp.int32, (ncp, w), 0)
    t = i * TQ + jnp.bitwise_and(lax.broadcasted_iota(jnp.int32, (ncp, w), 1), TQ - 1)
    valid = ((t - CMP_STRIDE * blk - (CMP_LEN - 1)) >= 0) & (blk < n_cmp)
    off = pl.multiple_of((nq - 1 - i) * (TQ // CMP_STRIDE), TQ // CMP_STRIDE)
    s = jnp.where(valid, _dot(kc, q) + bias_ref[0, 0, pl.ds(off, ncp), :], NEG)
    m = jnp.max(s, axis=0, keepdims=True)
    e = jnp.where(valid, jnp.exp(s - m), 0.0)
    l = jnp.sum(e, axis=0, keepdims=True)
    p = e / jnp.maximum(l, 1e-30)
    o_ref[0] = _heads_to_tokens(_dot(vct_ref[0, 0], p.astype(BF16)))
    psum = p[:, :TQ]
    for h in range(1, NSA_HPG):
        psum = psum + p[:, h * TQ:(h + 1) * TQ]
    hi, lo = _split_bf16(psum)
    ovt = ovt_ref[...]
    imp = _dot(ovt, hi) + _dot(ovt, lo)
    nsp = imp.shape[0]
    j = lax.broadcasted_iota(jnp.int32, (nsp, TQ), 0)
    cur = jnp.right_shift(i * TQ + lax.broadcasted_iota(jnp.int32, (nsp, TQ), 1), int(math.log2(SEL_LEN)))
    gap = cur - j
    forced = (j == 0) | ((gap >= 0) & (gap < N_LOCAL_SEL))
    allowed = j <= cur
    score = jnp.where(allowed, imp + SEL_FORCE_BONUS * forced.astype(F32), -jnp.inf)
    _, rank = _top_rows(score, n_top)
    sel = jnp.where((rank < float(n_top)) & allowed, 1.0, 0.0)
    if nsp < LANES:
        sel = jnp.concatenate([sel, jnp.zeros((LANES - nsp, TQ), F32)], axis=0)
    sel_ref[0, 0, 0] = sel.astype(sel_ref.dtype)


def _q_spec(col_q):
    wq = NSA_HPG * HEAD_DIM
    return pl.BlockSpec((1, TQ, wq), lambda bi, gi, i: (bi, i, col_q // wq + gi))


def _o_spec():
    return pl.BlockSpec((1, TQ, NSA_HPG * HEAD_DIM), lambda bi, gi, i: (bi, i, gi))


def _cmp_attention(proj, col_q, kc, vc_t, bias_c, overlap_t, n_cmp, n_top):
    b, t, _ = proj.shape
    g, ncp, d = kc.shape[1:]
    nq = t // TQ
    w = NSA_HPG * TQ
    return pl.pallas_call(
        functools.partial(_cmp_attn_body, n_cmp=n_cmp, n_top=n_top),
        grid=(b, g, nq),
        in_specs=[
            _q_spec(col_q),
            pl.BlockSpec((1, 1, ncp, d), lambda bi, gi, i: (bi, gi, 0, 0)),
            pl.BlockSpec((1, 1, d, ncp), lambda bi, gi, i: (bi, gi, 0, 0)),
            pl.BlockSpec((1, 1, bias_c.shape[2], w), lambda bi, gi, i: (gi, 0, 0, 0)),
            pl.BlockSpec(overlap_t.shape, lambda bi, gi, i: (0, 0)),
        ],
        out_specs=[
            _o_spec(),
            pl.BlockSpec((1, 1, 1, LANES, TQ), lambda bi, gi, i: (bi, gi, i, 0, 0)),
        ],
        out_shape=[jax.ShapeDtypeStruct((b, t, NSA_Q), F32), jax.ShapeDtypeStruct((b, g, nq, LANES, TQ), BF16)],
        compiler_params=_cparams(("parallel", "parallel", "parallel")),
        name="cmp_attention",
    )(proj, kc, vc_t, bias_c, overlap_t)


def _nsa_attn_body(q_ref, k_ref, vt_ref, bias_ref, *rest, selected, n_bias):
    if selected:
        sel_ref, exp_ref, o_ref, s_ref = rest
        sel4 = jnp.concatenate([sel_ref[0, 0, 0]] * NSA_HPG, axis=1)
    else:
        o_ref, s_ref = rest
    i = pl.program_id(2)
    q = _heads_on_lanes(q_ref[0])
    d, w = q.shape
    nq = vt_ref.shape[2]
    key = lax.broadcasted_iota(jnp.int32, (TQ, w), 0)
    qk = jnp.bitwise_and(lax.broadcasted_iota(jnp.int32, (TQ, w), 1), TQ - 1) - key

    def scores(kt, slot):
        kt_ld = jnp.clip(kt, 0, nq - 1)
        ks = k_ref[0, 0, pl.ds(pl.multiple_of(kt_ld * TQ, TQ), TQ), :]
        delta = i - kt
        rel = delta * TQ + qk
        if selected:
            ok = (_dot(exp_ref[kt_ld], sel4) > 0.5) & (rel >= 0)
        else:
            ok = (rel >= 0) & (rel < WINDOW)
        ok = ok & ((kt >= 0) & (kt <= i))
        s = jnp.where(ok, _dot(ks, q) + bias_ref[0, jnp.clip(delta, 0, n_bias - 1)], NEG)
        s_ref[slot] = s
        return _fold_rows(s, jnp.max)

    def weights(kt, slot, m):
        p = jnp.exp(s_ref[slot] - m)
        return _fold_rows(p, jnp.sum), _dot(vt_ref[0, 0, jnp.clip(kt, 0, nq - 1)], p.astype(BF16))

    mx = jnp.full((SUBLANES, w), NEG, F32)
    l8 = jnp.zeros((SUBLANES, w), F32)
    acc = jnp.zeros((d, w), F32)
    if selected:
        n_it = (i + NSA_TILES_PER_ITER) // NSA_TILES_PER_ITER

        def pass1(it, mx):
            for u in range(NSA_TILES_PER_ITER):
                kt = it * NSA_TILES_PER_ITER + u
                mx = jnp.maximum(mx, scores(kt, kt))
            return mx

        m = jnp.max(lax.fori_loop(0, n_it, pass1, mx), axis=0, keepdims=True)

        def pass2(it, carry):
            l8, acc = carry
            for u in range(NSA_TILES_PER_ITER):
                kt = it * NSA_TILES_PER_ITER + u
                dl, da = weights(kt, kt, m)
                l8, acc = l8 + dl, acc + da
            return l8, acc

        l8, acc = lax.fori_loop(0, n_it, pass2, (l8, acc))
    else:
        n_win = s_ref.shape[0]
        for u in range(n_win):
            mx = jnp.maximum(mx, scores(i - (n_win - 1) + u, u))
        m = jnp.max(mx, axis=0, keepdims=True)
        for u in range(n_win):
            dl, da = weights(i - (n_win - 1) + u, u, m)
            l8, acc = l8 + dl, acc + da
    o_ref[0] = _heads_to_tokens(acc / jnp.sum(l8, axis=0, keepdims=True))


def _nsa_attention(proj, col_q, k, v_t, bias_t, sel=None, expand_t=None):
    b, g, t, d = k.shape
    nq = t // TQ
    w = NSA_HPG * TQ
    n_bias = bias_t.shape[1]
    selected = sel is not None
    in_specs = [
        _q_spec(col_q),
        pl.BlockSpec((1, 1, t, d), lambda bi, gi, i: (bi, gi, 0, 0)),
        pl.BlockSpec((1, 1, nq, d, TQ), lambda bi, gi, i: (bi, gi, 0, 0, 0)),
        pl.BlockSpec((1, n_bias, TQ, w), lambda bi, gi, i: (gi, 0, 0, 0)),
    ]
    args = [proj, k, v_t, bias_t]
    if selected:
        in_specs += [pl.BlockSpec((1, 1, 1, LANES, TQ), lambda bi, gi, i: (bi, gi, i, 0, 0)),
                     pl.BlockSpec(expand_t.shape, lambda bi, gi, i: (0, 0, 0))]
        args += [sel, expand_t]
    return pl.pallas_call(
        functools.partial(_nsa_attn_body, selected=selected, n_bias=n_bias),
        grid=(b, g, nq),
        in_specs=in_specs,
        out_specs=_o_spec(),
        out_shape=jax.ShapeDtypeStruct((b, t, NSA_Q), F32),
        scratch_shapes=[pltpu.VMEM((nq + NSA_TILES_PER_ITER - 1 if selected else WINDOW // TQ + 1, TQ, w), F32)],
        compiler_params=_cparams(("parallel", "parallel", "parallel")),
        name="sel_attention" if selected else "win_attention",
    )(*args)


def _sb_body(q_ref, k_ref, v_ref, uo_ref, o_ref, c_ref):
    i = pl.program_id(2)
    uo = uo_ref[...]
    hb = c_ref.shape[0]
    pairs = range(hb // 2)
    row = lax.broadcasted_iota(jnp.int32, (TQ, TQ), 0)
    col = lax.broadcasted_iota(jnp.int32, (TQ, TQ), 1)
    rc = row - col
    first = col < HEAD_DIM
    c_ref[...] = jnp.zeros_like(c_ref)
    o_ref[...] = jnp.zeros_like(o_ref)
    qs = []
    for p in pairs:
        qp = q_ref[0, :, p * LANES:(p + 1) * LANES]
        zero = jnp.zeros_like(qp)
        qs += [jnp.where(first, qp, zero), jnp.where(first, zero, qp)]

    def cond(st):
        kt, cmax = st
        return (kt >= 0) & (cmax > SB_SKIP_LOG)

    def body(st):
        kt, _ = st
        start = pl.multiple_of(kt * TQ, TQ)
        mask = ((i - kt) * TQ + rc) > 0
        heads = range(hb)
        ks = [k_ref[0, pl.ds(start, TQ), p * LANES:(p + 1) * LANES] for p in pairs]
        vs = [v_ref[0, pl.ds(start, TQ), p * LANES:(p + 1) * LANES] for p in pairs]
        zs = [_dot_nt(qs[h], ks[h // 2]) for h in heads]
        lbs = [jnp.minimum(z, 0.0) - jnp.log(1.0 + jnp.exp(-jnp.abs(z))) for z in zs]
        parts = [_split_bf16(jnp.where(mask, lbs[h] - zs[h], 0.0)) for h in heads]
        sfxs = [_dot(hi, uo) + _dot(lo, uo) for hi, lo in parts]
        ws = [jnp.where(mask, jnp.exp(lbs[h] + sfxs[h][:, :TQ] + c_ref[h]), 0.0).astype(BF16) for h in heads]
        cm = None
        for p in pairs:
            o_ref[0, :, p * LANES:(p + 1) * LANES] += jnp.where(first, _dot(ws[2 * p], vs[p]),
                                                                _dot(ws[2 * p + 1], vs[p]))
        for h in heads:
            c = c_ref[h] + sfxs[h][:, TQ:]
            c_ref[h] = c
            cm = c if cm is None else jnp.maximum(cm, c)
        return kt - 1, jnp.max(cm)

    lax.while_loop(cond, body, (i, jnp.float32(0.0)))


def _stick_breaking(proj, col_q, col_k, col_v):
    b, t, _ = proj.shape
    hb = SB_HEADS_PER_STEP
    wb = hb * HEAD_DIM
    tri = np.triu(np.ones((TQ, TQ), np.float32), 0).T - np.eye(TQ, dtype=np.float32)
    uo = jnp.asarray(np.concatenate([tri, np.ones((TQ, TQ), np.float32)], axis=1), BF16)
    return pl.pallas_call(
        _sb_body,
        grid=(b, SB_HEADS // hb, t // TQ),
        in_specs=[
            pl.BlockSpec((1, TQ, wb), lambda bi, hi, i: (bi, i, col_q // wb + hi)),
            pl.BlockSpec((1, t, wb), lambda bi, hi, i: (bi, 0, col_k // wb + hi)),
            pl.BlockSpec((1, t, wb), lambda bi, hi, i: (bi, 0, col_v // wb + hi)),
            pl.BlockSpec((TQ, 2 * TQ), lambda bi, hi, i: (0, 0)),
        ],
        out_specs=pl.BlockSpec((1, TQ, wb), lambda bi, hi, i: (bi, i, hi)),
        out_shape=jax.ShapeDtypeStruct((b, t, SB_W), F32),
        scratch_shapes=[pltpu.VMEM((hb, TQ, TQ), F32)],
        compiler_params=_cparams(("parallel", "parallel", "parallel")),
        name="stick_breaking",
    )(proj, proj, proj, uo)


def _merge_body(oc_ref, os_ref, ow_ref, osb_ref, gbr_ref, ga_ref, gb_ref, e_ref, wn_ref, wsb_ref, o_ref):
    gate = jax.nn.sigmoid(gbr_ref[...].astype(F32))
    hi, lo = _split_bf16(gate)
    e = e_ref[...]
    gexp = _dot(hi, e) + _dot(lo, e)
    o_nsa = (gexp[:, :NSA_Q] * oc_ref[...] + gexp[:, NSA_Q:2 * NSA_Q] * os_ref[...]
             + gexp[:, 2 * NSA_Q:] * ow_ref[...])
    a = _dot(o_nsa.astype(BF16), wn_ref[...])
    bm = _dot(osb_ref[...].astype(BF16), wsb_ref[...])
    merged = jax.nn.sigmoid(ga_ref[...].astype(F32)) * a + jax.nn.sigmoid(gb_ref[...].astype(F32)) * bm
    o_ref[...] = merged.astype(o_ref.dtype)


def _merge(o_c, o_s, o_w, o_sb, proj, col_gbr, col_ga, col_gb, w_nsa, w_sb, tm=256):
    n = o_c.shape[0]
    dm = w_nsa.shape[1]
    e = np.zeros((LANES, 3 * NSA_Q), np.float32)
    for j in range(3 * NSA_HEADS):
        e[j, j * HEAD_DIM:(j + 1) * HEAD_DIM] = 1.0
    row = lambda i: (i, 0)
    full = lambda i: (0, 0)
    return pl.pallas_call(
        _merge_body,
        grid=(n // tm,),
        in_specs=[
            pl.BlockSpec((tm, NSA_Q), row), pl.BlockSpec((tm, NSA_Q), row), pl.BlockSpec((tm, NSA_Q), row),
            pl.BlockSpec((tm, SB_W), row),
            pl.BlockSpec((tm, LANES), lambda i: (i, col_gbr // LANES)),
            pl.BlockSpec((tm, dm), lambda i: (i, col_ga // dm)),
            pl.BlockSpec((tm, dm), lambda i: (i, col_gb // dm)),
            pl.BlockSpec((LANES, 3 * NSA_Q), full),
            pl.BlockSpec((NSA_Q, dm), full), pl.BlockSpec((SB_W, dm), full),
        ],
        out_specs=pl.BlockSpec((tm, dm), row),
        out_shape=jax.ShapeDtypeStruct((n, dm), BF16),
        compiler_params=_cparams(("parallel",)),
        name="gated_merge",
    )(o_c, o_s, o_w, o_sb, proj, proj, proj, jnp.asarray(e, BF16), w_nsa.astype(BF16), w_sb.astype(BF16))


def _peer_scores_body(wqt_ref, xt_ref, keys_ref, s_ref):
    qt = _dot(wqt_ref[...], xt_ref[...])
    c = keys_ref.shape[-1]
    for ch in range(keys_ref.shape[0]):
        qh, ql = _split_bf16(qt[ch * c:(ch + 1) * c])
        kh, kl = _split_bf16(keys_ref[ch])
        s_ref[ch * PEER_KEYS:(ch + 1) * PEER_KEYS, :] = _dot(kh, qh) + _dot(kh, ql) + _dot(kl, qh)


def _peer_scores(wq_t, x_t, keys, tn=512):
    rq, d = wq_t.shape
    n = x_t.shape[1]
    nch, nk, c = keys.shape
    return pl.pallas_call(
        _peer_scores_body,
        grid=(n // tn,),
        in_specs=[pl.BlockSpec((rq, d), lambda i: (0, 0)), pl.BlockSpec((d, tn), lambda i: (0, i)),
                  pl.BlockSpec((nch, nk, c), lambda i: (0, 0, 0))],
        out_specs=pl.BlockSpec((nch * nk, tn), lambda i: (0, i)),
        out_shape=jax.ShapeDtypeStruct((nch * nk, n), F32),
        compiler_params=_cparams(("parallel",)),
        name="peer_scores",
    )(wq_t, x_t, keys)


def _peer_route_body(s_ref, lim0_ref, c0_ref, rank1_ref, e1_ref):
    k = PEER_TOPK
    tn = s_ref.shape[-1]
    kidx = lax.broadcasted_iota(jnp.int32, (k, tn), 0)
    flat = lax.broadcasted_iota(jnp.int32, (k * k, tn), 0)
    reachable = (jnp.right_shift(flat, int(math.log2(k))) + 1) * (jnp.bitwise_and(flat, k - 1) + 1) <= k
    for h in range(PEER_HEADS):
        s0 = s_ref[(2 * h) * PEER_KEYS:(2 * h + 1) * PEER_KEYS, :]
        s1 = s_ref[(2 * h + 1) * PEER_KEYS:(2 * h + 2) * PEER_KEYS, :]
        a, rank0 = _top_rows(s0, k)
        b, rank1 = _top_rows(s1, k)
        cand = jnp.concatenate([a[r:r + 1] + b for r in range(k)], axis=0)
        cand = jnp.where(reachable, cand, -jnp.inf)
        best, crank = _top_rows(cand, k)
        chosen = jnp.where(crank < float(k), 1.0, 0.0)
        e = chosen * jnp.exp(jnp.minimum(cand - best[0:1], 0.0))
        z = jnp.sum(e, axis=0, keepdims=True)
        cnt = jnp.zeros((k, tn), F32)
        for r in range(k):
            n_r = jnp.sum(chosen[r * k:(r + 1) * k], axis=0, keepdims=True)
            cnt = jnp.where(kidx == r, n_r, cnt)
        lim0 = jnp.zeros_like(s0)
        for r in range(k):
            lim0 = jnp.where(rank0 == float(r), cnt[r:r + 1], lim0)
        lim0_ref[h] = lim0
        c0_ref[h] = jnp.exp(s0 - a[0:1]) / z
        rank1_ref[h] = rank1.astype(rank1_ref.dtype)
        e1_ref[h] = jnp.exp(s1 - b[0:1]).astype(e1_ref.dtype)


def _peer_route(s_t, tn=256):
    rows, n = s_t.shape
    shp = jax.ShapeDtypeStruct((PEER_HEADS, PEER_KEYS, n), F32)
    shp16 = jax.ShapeDtypeStruct((PEER_HEADS, PEER_KEYS, n), BF16)
    spec = pl.BlockSpec((PEER_HEADS, PEER_KEYS, tn), lambda i: (0, 0, i))
    return pl.pallas_call(
        _peer_route_body,
        grid=(n // tn,),
        in_specs=[pl.BlockSpec((rows, tn), lambda i: (0, i))],
        out_specs=[spec, spec, spec, spec],
        out_shape=[shp, shp, shp16, shp16],
        compiler_params=_cparams(("parallel",)),
        name="peer_route",
    )(s_t)


def _peer_main_body(u_ref, vt_ref, xt_ref, lim0_ref, c0_ref, rank1_ref, e1_ref, o_ref, act_ref, w_ref, *, te, ts):
    et = pl.program_id(1)

    @pl.when(et == 0)
    def _():
        o_ref[...] = jnp.zeros_like(o_ref)

    xt = xt_ref[...]
    tn = xt.shape[1]
    n_i = ts // PEER_KEYS
    for sb in range(te // ts):
        act_ref[sb] = _dot(u_ref[sb * ts:(sb + 1) * ts, :], xt)
    pk = 2 * SUBLANES
    grp = (PEER_KEYS // pk, pk, LANES)
    for sb in range(te // ts):
        for ii in range(n_i):
            r = sb * n_i + ii
            rows = slice(ii * PEER_KEYS, (ii + 1) * PEER_KEYS)
            for lc in range(tn // LANES):
                cols = slice(lc * LANES, (lc + 1) * LANES)
                s = jnp.zeros(grp, BF16)
                for h in range(PEER_HEADS):
                    lim = jnp.broadcast_to(lim0_ref[h, r:r + 1, cols], (pk, LANES)).astype(BF16)
                    cc = jnp.broadcast_to(c0_ref[h, r:r + 1, cols], (pk, LANES)).astype(BF16)
                    s = s + jnp.where(rank1_ref[h, :, cols].reshape(grp) < lim[None],
                                      e1_ref[h, :, cols].reshape(grp) * cc[None], jnp.zeros((), BF16))
                g = jax.nn.gelu(act_ref[sb, rows, cols]).astype(BF16)
                w_ref[sb, rows, cols] = s.reshape(PEER_KEYS, LANES) * g
        o_ref[...] += _dot(vt_ref[0, :, sb * ts:(sb + 1) * ts], w_ref[sb])


def _peer_main(u, v_t, x_t, lim0, c0, rank1, e1, tn=512, te=1024, ts=256):
    n_exp, d = u.shape
    n = x_t.shape[1]
    tab = pl.BlockSpec((PEER_HEADS, PEER_KEYS, tn), lambda i, j: (0, 0, i))
    tab0 = pl.BlockSpec((PEER_HEADS, te // PEER_KEYS, tn), lambda i, j: (0, j, i))
    return pl.pallas_call(
        functools.partial(_peer_main_body, te=te, ts=ts),
        grid=(n // tn, n_exp // te),
        in_specs=[pl.BlockSpec((te, d), lambda i, j: (j, 0)), pl.BlockSpec((1, d, te), lambda i, j: (j, 0, 0)),
                  pl.BlockSpec((d, tn), lambda i, j: (0, i)), tab0, tab0, tab, tab],
        out_specs=pl.BlockSpec((d, tn), lambda i, j: (0, i)),
        out_shape=jax.ShapeDtypeStruct((d, n), F32),
        scratch_shapes=[pltpu.VMEM((te // ts, ts, tn), F32), pltpu.VMEM((te // ts, ts, tn), BF16)],
        compiler_params=_cparams(("parallel", "arbitrary"), vmem_mb=56),
        name="peer_experts",
    )(u, v_t, x_t, lim0, c0, rank1, e1)


def kernel(x, attn_norm_g, w_in, cmp_k_pe, cmp_k_w1, cmp_k_w2, cmp_v_pe, cmp_v_w1, cmp_v_w2, rel_bias_table,
           w_branch_nsa, w_branch_sb, w_out, ffn_norm_g, peer_w_q, peer_sub_keys, peer_u, peer_v, final_norm_g):
    h = x
    for l in range(attn_norm_g.shape[0]):
        h = _layer(h, attn_norm_g[l], w_in[l], cmp_k_pe[l], cmp_k_w1[l], cmp_k_w2[l], cmp_v_pe[l], cmp_v_w1[l],
                   cmp_v_w2[l], rel_bias_table, w_branch_nsa[l], w_branch_sb[l], w_out[l], ffn_norm_g[l],
                   peer_w_q[l], peer_sub_keys[l], peer_u[l], peer_v[l],
                   final_norm_g if l == attn_norm_g.shape[0] - 1 else None)
    return h


def _layer(h, attn_g, w_in, ck_pe, ck_w1, ck_w2, cv_pe, cv_w1, cv_w2, rel_table, w_br_nsa, w_br_sb, w_out,
           ffn_g, pq, psk, pu, pv, final_g):
    b, t, dm = h.shape
    n = b * t
    g, hg, d = NSA_GROUPS, NSA_HPG, HEAD_DIM
    nq = t // TQ
    x2 = h.reshape(n, dm)

    o_gbr = NSA_Q + 6 * NSA_KV
    n_gbr = 3 * NSA_HEADS
    o_qb = o_gbr + n_gbr
    o_ga = o_qb + 3 * SB_W
    tn_in = 768
    packed = w_in.shape[1]
    packed_pad = -(-packed // tn_in) * tn_in
    scale = HEAD_DIM ** -0.5
    w_pack = jnp.concatenate([w_in[:, o_ga:], w_in[:, :NSA_Q] * scale, w_in[:, NSA_Q:o_gbr],
                              w_in[:, o_qb:o_qb + SB_W] * scale, w_in[:, o_qb + SB_W:o_ga], w_in[:, o_gbr:o_qb],
                              jnp.zeros((dm, packed_pad - packed), w_in.dtype)], axis=1).astype(BF16)
    a = _rmsnorm(x2, attn_g, BF16)
    proj = _matmul(a, w_pack, 512, tn_in, BF16, name="in_proj")
    proj3 = proj.reshape(b, t, packed_pad)
    col_ga, col_gb = 0, dm
    col_qn = 2 * dm
    col_kv = col_qn + NSA_Q
    col_qb = col_kv + 6 * NSA_KV
    col_gbr = col_qb + 3 * SB_W

    def heads_kv(z):
        return z.reshape(b, t, g, d).transpose(0, 2, 1, 3)

    def tiles_kv_t(z):
        return z.reshape(b, nq, TQ, g, d).transpose(0, 3, 1, 4, 2)

    kv_cols = [proj[:, col_kv + j * NSA_KV:col_kv + (j + 1) * NSA_KV] for j in range(6)]
    kc_tok, vc_tok = heads_kv(kv_cols[0]), heads_kv(kv_cols[1])
    ks, vs_t = heads_kv(kv_cols[2]), tiles_kv_t(kv_cols[3])
    kw, vw_t = heads_kv(kv_cols[4]), tiles_kv_t(kv_cols[5])

    n_cmp = (t - CMP_LEN) // CMP_STRIDE + 1
    n_chunk = t // CMP_STRIDE
    ncp = -(-n_cmp // LANES) * LANES
    reps = CMP_LEN // CMP_STRIDE

    def blocks(tok):
        ch = tok.reshape(b, g, n_chunk, CMP_STRIDE * d)
        ch = jnp.pad(ch, ((0, 0), (0, 0), (0, ncp + reps - 1 - n_chunk), (0, 0)))
        blk = jnp.concatenate([ch[:, :, r:r + ncp] for r in range(reps)], axis=-1)
        return blk.reshape(b * g * ncp, CMP_LEN * d)

    kc_blk = _compress(blocks(kc_tok), ck_pe.reshape(1, CMP_LEN * d), ck_w1, ck_w2, ncp).reshape(b, g, ncp, d)
    vc_blk = _compress(blocks(vc_tok), cv_pe.reshape(1, CMP_LEN * d), cv_w1, cv_w2, ncp).reshape(b, g, ncp, d)

    r_i = jnp.arange(TQ, dtype=jnp.int32)
    per_tile = TQ // CMP_STRIDE
    rows_c = ncp + per_tile * (nq - 1)
    shift_c = per_tile * (nq - 1) - jnp.arange(rows_c, dtype=jnp.int32)
    dist_c = CMP_STRIDE * shift_c[None, :, None] + r_i[None, None, :] - (CMP_LEN - 1)
    bias_c = _bias_expand(rel_table, _rel_bucket(dist_c))
    n_bias = min(nq, -(-(REL_MAX_DIST + TQ - 1) // TQ) + 1)
    dist_t = jnp.arange(n_bias, dtype=jnp.int32)[:, None, None] * TQ + r_i[None, None, :] - r_i[None, :, None]
    bias_t = _bias_expand(rel_table, _rel_bucket(dist_t))

    n_sel = t // SEL_LEN
    nsp = min(LANES, -(-n_sel // 16) * 16)
    c_start = np.arange(ncp) * CMP_STRIDE
    s_start = np.arange(nsp) * SEL_LEN
    overlap_t = np.maximum(np.minimum(c_start[None, :] + CMP_LEN, s_start[:, None] + SEL_LEN)
                           - np.maximum(c_start[None, :], s_start[:, None]), 0).astype(np.float32) / CMP_LEN
    overlap_t[:, n_cmp:] = 0.0
    overlap_t[n_sel:, :] = 0.0
    o_c, sel = _cmp_attention(proj3, col_qn, kc_blk, vc_blk.transpose(0, 1, 3, 2), bias_c,
                              jnp.asarray(overlap_t, BF16), n_cmp, min(SEL_TOPK, n_sel))
    key_blk = (np.arange(t) // SEL_LEN).reshape(nq, TQ, 1)
    expand_t = (key_blk == np.arange(LANES)[None, None, :]).astype(np.float32)
    o_s = _nsa_attention(proj3, col_qn, ks, vs_t, bias_t, sel, jnp.asarray(expand_t, BF16))
    o_w = _nsa_attention(proj3, col_qn, kw, vw_t, bias_t)
    o_sb = _stick_breaking(proj3, col_qb, col_qb + SB_W, col_qb + 2 * SB_W)

    merged = _merge(o_c.reshape(n, NSA_Q), o_s.reshape(n, NSA_Q), o_w.reshape(n, NSA_Q), o_sb.reshape(n, SB_W),
                    proj, col_gbr, col_ga, col_gb, w_br_nsa, w_br_sb)
    h1 = _matmul(merged, w_out.astype(BF16), 512, 1024, F32, res=x2, name="out_proj")

    xn_t = _rmsnorm(h1, ffn_g, BF16).T
    keys = psk.reshape(PEER_HEADS * 2, PEER_KEYS, -1)
    s_t = _peer_scores(pq.T.astype(BF16), xn_t, keys)
    lim0, c0, rank1, e1 = _peer_route(s_t)
    te = 1024
    pv_t = pv.reshape(-1, te, dm).transpose(0, 2, 1).astype(BF16)
    ffn_t = _peer_main(pu.astype(BF16), pv_t, xn_t, lim0, c0, rank1, e1, te=te)
    ffn = ffn_t.T
    if final_g is None:
        return (h1 + ffn).reshape(b, t, dm)
    return _add_rmsnorm(h1, ffn, final_g).reshape(b, t, dm)
```

```python
import functools
import math

import jax
import jax.numpy as jnp
import numpy as np
from jax import lax
from jax.experimental import pallas as pl
from jax.experimental.pallas import tpu as pltpu

F32 = jnp.float32
BF16 = jnp.bfloat16

HEAD_DIM = 64
NSA_HEADS = 16
NSA_GROUPS = 4
NSA_HPG = NSA_HEADS // NSA_GROUPS
SB_HEADS = 16
CMP_LEN = 32
CMP_STRIDE = 16
SEL_LEN = 64
SEL_TOPK = 16
N_LOCAL_SEL = 2
SEL_FORCE_BONUS = 1e4
WINDOW = 512
REL_BUCKETS = 32
REL_MAX_DIST = 1024
PEER_HEADS = 8
PEER_KEYS = 128
PEER_TOPK = 16
EPS = 1e-6
NEG = -1e30

NSA_Q = NSA_HEADS * HEAD_DIM
NSA_KV = NSA_GROUPS * HEAD_DIM
SB_W = SB_HEADS * HEAD_DIM

LANES = 128
SUBLANES = 8
TQ = 128
NSA_TILES_PER_ITER = 4
SB_HEADS_PER_STEP = 8
SB_SKIP_LOG = -110.0
PEER_CAND_ROWS = 4
PEER_CAND_COLS = 3
assert all(r0 < PEER_CAND_ROWS or r1 < PEER_CAND_COLS for r0 in range(PEER_TOPK) for r1 in range(PEER_TOPK)
           if (r0 + 1) * (r1 + 1) <= PEER_TOPK)

_NT = (((1,), (1,)), ((), ()))


def _cparams(sem, vmem_mb=48):
    return pltpu.CompilerParams(dimension_semantics=sem, vmem_limit_bytes=vmem_mb * 1024 * 1024)


def _dot(a, b):
    return jnp.dot(a, b, preferred_element_type=F32)


def _dot_nt(a, b):
    return lax.dot_general(a, b, _NT, preferred_element_type=F32)


def _split_bf16(x):
    hi = x.astype(BF16)
    lo = (x - hi.astype(F32)).astype(BF16)
    return hi, lo


def _fold_rows(x, op):
    r, l = x.shape
    return op(x.reshape(r // SUBLANES, SUBLANES, l), axis=0)


def _rmsnorm_body(x_ref, g_ref, o_ref):
    x = x_ref[...]
    y = x * lax.rsqrt(jnp.mean(x * x, axis=-1, keepdims=True) + EPS)
    o_ref[...] = (y * g_ref[...]).astype(o_ref.dtype)


def _rmsnorm(x, g, out_dtype, tm=512):
    n, d = x.shape
    return pl.pallas_call(
        _rmsnorm_body,
        grid=(n // tm,),
        in_specs=[pl.BlockSpec((tm, d), lambda i: (i, 0)), pl.BlockSpec((1, d), lambda i: (0, 0))],
        out_specs=pl.BlockSpec((tm, d), lambda i: (i, 0)),
        out_shape=jax.ShapeDtypeStruct((n, d), out_dtype),
        compiler_params=_cparams(("parallel",)),
        name="rmsnorm",
    )(x, g.reshape(1, d))


def _add_rmsnorm_body(x_ref, y_ref, g_ref, o_ref):
    x = x_ref[...] + y_ref[...]
    y = x * lax.rsqrt(jnp.mean(x * x, axis=-1, keepdims=True) + EPS)
    o_ref[...] = (y * g_ref[...]).astype(o_ref.dtype)


def _add_rmsnorm(x, y, g, tm=512):
    n, d = x.shape
    return pl.pallas_call(
        _add_rmsnorm_body,
        grid=(n // tm,),
        in_specs=[pl.BlockSpec((tm, d), lambda i: (i, 0)), pl.BlockSpec((tm, d), lambda i: (i, 0)),
                  pl.BlockSpec((1, d), lambda i: (0, 0))],
        out_specs=pl.BlockSpec((tm, d), lambda i: (i, 0)),
        out_shape=jax.ShapeDtypeStruct((n, d), F32),
        compiler_params=_cparams(("parallel",)),
        name="add_rmsnorm",
    )(x, y, g.reshape(1, d))


def _mm_body(a_ref, w_ref, o_ref):
    o_ref[...] = _dot(a_ref[...], w_ref[...]).astype(o_ref.dtype)


def _mm_res_body(a_ref, w_ref, r_ref, o_ref):
    o_ref[...] = (r_ref[...] + _dot(a_ref[...], w_ref[...])).astype(o_ref.dtype)


def _matmul(a, w, tm, tn, out_dtype, res=None, name="matmul"):
    m, k = a.shape
    nc = w.shape[1]
    in_specs = [pl.BlockSpec((tm, k), lambda j, i: (i, 0)), pl.BlockSpec((k, tn), lambda j, i: (0, j))]
    args = [a, w]
    body = _mm_body
    if res is not None:
        in_specs.append(pl.BlockSpec((tm, tn), lambda j, i: (i, j)))
        args.append(res)
        body = _mm_res_body
    return pl.pallas_call(
        body,
        grid=(nc // tn, m // tm),
        in_specs=in_specs,
        out_specs=pl.BlockSpec((tm, tn), lambda j, i: (i, j)),
        out_shape=jax.ShapeDtypeStruct((m, nc), out_dtype),
        compiler_params=_cparams(("parallel", "parallel")),
        name=name,
    )(*args)


def _rel_bucket(dist):
    dist = jnp.maximum(dist, 0)
    n_exact = REL_BUCKETS // 2
    d_f = jnp.maximum(dist, 1).astype(jnp.float32)
    large = n_exact + (jnp.log(d_f / n_exact) / math.log(REL_MAX_DIST / n_exact)
                       * (REL_BUCKETS - n_exact)).astype(jnp.int32)
    large = jnp.minimum(large, REL_BUCKETS - 1)
    return jnp.bitwise_and(jnp.where(dist < n_exact, dist, large), REL_BUCKETS - 1)


def _bias_expand_body(tab_ref, bkt_ref, o_ref):
    h = pl.program_id(1)
    bkt = bkt_ref[0]
    acc = jnp.full(bkt.shape, tab_ref[0, h], F32)
    for k in range(1, REL_BUCKETS):
        acc = jnp.where(bkt >= k, tab_ref[k, h], acc)
    o_ref[0, 0] = acc


def _bias_expand(table, bucket):
    r, rows, _ = bucket.shape
    return pl.pallas_call(
        _bias_expand_body,
        grid=(r, NSA_HEADS),
        in_specs=[pl.BlockSpec(memory_space=pltpu.SMEM), pl.BlockSpec((1, rows, TQ), lambda i, h: (i, 0, 0))],
        out_specs=pl.BlockSpec((1, 1, rows, TQ), lambda i, h: (h // NSA_HPG, i, 0, h % NSA_HPG)),
        out_shape=jax.ShapeDtypeStruct((NSA_GROUPS, r, rows, NSA_HPG * TQ), F32),
        compiler_params=_cparams(("parallel", "parallel")),
        name="bias_expand",
    )(table, bucket)


def _top_rows(x, k, order=None):
    r, l = x.shape
    ridx = lax.broadcasted_iota(jnp.int32, (r, l), 0).astype(F32) if order is None else order
    kidx = lax.broadcasted_iota(jnp.int32, (k, l), 0)
    rank = jnp.full((r, l), float(k), F32)
    vals = jnp.zeros((k, l), F32)
    cur = x
    for j in range(k):
        m = jnp.max(cur, axis=0, keepdims=True)
        idx = jnp.min(jnp.where(cur == m, ridx, jnp.inf), axis=0, keepdims=True)
        pick = ridx == idx
        rank = jnp.where(pick, float(j), rank)
        cur = jnp.where(pick, -jnp.inf, cur)
        vals = jnp.where(kidx == j, m, vals)
    return vals, rank


def _compress_body(blk_ref, pe_ref, w1_ref, w2_ref, o_ref):
    blk = (blk_ref[...].astype(F32) + pe_ref[...]).astype(BF16)
    h = jax.nn.gelu(_dot(blk, w1_ref[...]))
    o_ref[...] = _dot(h.astype(BF16), w2_ref[...]).astype(o_ref.dtype)


def _compress(blk, pe_flat, w1, w2, tr):
    r, kd = blk.shape
    hid = w1.shape[1]
    d = w2.shape[1]
    return pl.pallas_call(
        _compress_body,
        grid=(r // tr,),
        in_specs=[pl.BlockSpec((tr, kd), lambda i: (i, 0)), pl.BlockSpec((1, kd), lambda i: (0, 0)),
                  pl.BlockSpec((kd, hid), lambda i: (0, 0)), pl.BlockSpec((hid, d), lambda i: (0, 0))],
        out_specs=pl.BlockSpec((tr, d), lambda i: (i, 0)),
        out_shape=jax.ShapeDtypeStruct((r, d), BF16),
        compiler_params=_cparams(("parallel",)),
        name="compress_mlp",
    )(blk, pe_flat, w1.astype(BF16), w2.astype(BF16))


def _heads_on_lanes(q_blk):
    qt = q_blk.astype(F32).T
    d = qt.shape[0] // NSA_HPG
    return jnp.concatenate([qt[h * d:(h + 1) * d] for h in range(NSA_HPG)], axis=1).astype(BF16)


def _heads_to_tokens(o):
    return jnp.concatenate([o[:, h * TQ:(h + 1) * TQ] for h in range(NSA_HPG)], axis=0).T


def _cmp_attn_body(q_ref, kc_ref, vct_ref, bias_ref, ovt_ref, o_ref, sel_ref, *, n_cmp, n_top):
    i = pl.program_id(2)
    nq = pl.num_programs(2)
    q = _heads_on_lanes(q_ref[0])
    kc = kc_ref[0, 0]
    ncp, w = kc.shape[0], q.shape[1]
    blk = lax.broadcasted_iota(jnp.int32, (ncp, w), 0)
    t = i * TQ + jnp.bitwise_and(lax.broadcasted_iota(jnp.int32, (ncp, w), 1), TQ - 1)
    valid = ((t - CMP_STRIDE * blk - (CMP_LEN - 1)) >= 0) & (blk < n_cmp)
    off = pl.multiple_of((nq - 1 - i) * (TQ // CMP_STRIDE), TQ // CMP_STRIDE)
    s = jnp.where(valid, _dot(kc, q) + bias_ref[0, 0, pl.ds(off, ncp), :], NEG)
    m = jnp.max(s, axis=0, keepdims=True)
    e = jnp.where(valid, jnp.exp(s - m), 0.0)
    l = jnp.sum(e, axis=0, keepdims=True)
    p = e / jnp.maximum(l, 1e-30)
    o_ref[0] = _heads_to_tokens(_dot(vct_ref[0, 0], p.astype(BF16)))
    psum = p[:, :TQ]
    for h in range(1, NSA_HPG):
        psum = psum + p[:, h * TQ:(h + 1) * TQ]
    hi, lo = _split_bf16(psum)
    ovt = ovt_ref[...]
    imp = _dot(ovt, hi) + _dot(ovt, lo)
    nsp = imp.shape[0]
    j = lax.broadcasted_iota(jnp.int32, (nsp, TQ), 0)
    cur = jnp.right_shift(i * TQ + lax.broadcasted_iota(jnp.int32, (nsp, TQ), 1), int(math.log2(SEL_LEN)))
    gap = cur - j
    forced = (j == 0) | ((gap >= 0) & (gap < N_LOCAL_SEL))
    allowed = j <= cur
    score = jnp.where(allowed, imp + SEL_FORCE_BONUS * forced.astype(F32), -jnp.inf)
    _, rank = _top_rows(score, n_top)
    sel = jnp.where((rank < float(n_top)) & allowed, 1.0, 0.0)
    if nsp < LANES:
        sel = jnp.concatenate([sel, jnp.zeros((LANES - nsp, TQ), F32)], axis=0)
    sel_ref[0, 0, 0] = sel.astype(sel_ref.dtype)


def _q_spec(col_q):
    wq = NSA_HPG * HEAD_DIM
    return pl.BlockSpec((1, TQ, wq), lambda bi, gi, i: (bi, i, col_q // wq + gi))


def _o_spec():
    return pl.BlockSpec((1, TQ, NSA_HPG * HEAD_DIM), lambda bi, gi, i: (bi, i, gi))


def _cmp_attention(proj, col_q, kc, vc_t, bias_c, overlap_t, n_cmp, n_top):
    b, t, _ = proj.shape
    g, ncp, d = kc.shape[1:]
    nq = t // TQ
    w = NSA_HPG * TQ
    return pl.pallas_call(
        functools.partial(_cmp_attn_body, n_cmp=n_cmp, n_top=n_top),
        grid=(b, g, nq),
        in_specs=[
            _q_spec(col_q),
            pl.BlockSpec((1, 1, ncp, d), lambda bi, gi, i: (bi, gi, 0, 0)),
            pl.BlockSpec((1, 1, d, ncp), lambda bi, gi, i: (bi, gi, 0, 0)),
            pl.BlockSpec((1, 1, bias_c.shape[2], w), lambda bi, gi, i: (gi, 0, 0, 0)),
            pl.BlockSpec(overlap_t.shape, lambda bi, gi, i: (0, 0)),
        ],
        out_specs=[
            _o_spec(),
            pl.BlockSpec((1, 1, 1, LANES, TQ), lambda bi, gi, i: (bi, gi, i, 0, 0)),
        ],
        out_shape=[jax.ShapeDtypeStruct((b, t, NSA_Q), F32), jax.ShapeDtypeStruct((b, g, nq, LANES, TQ), BF16)],
        compiler_params=_cparams(("parallel", "parallel", "parallel")),
        name="cmp_attention",
    )(proj, kc, vc_t, bias_c, overlap_t)


def _nsa_attn_body(q_ref, k_ref, vt_ref, bias_ref, *rest, selected, n_bias):
    if selected:
        sel_ref, o_ref, s_ref, selv_ref = rest
        selv_ref[...] = jnp.concatenate([sel_ref[0, 0, 0].astype(F32)] * NSA_HPG, axis=1)
    else:
        o_ref, s_ref = rest
    i = pl.program_id(2)
    q = _heads_on_lanes(q_ref[0])
    d, w = q.shape
    nq = vt_ref.shape[2]
    key = lax.broadcasted_iota(jnp.int32, (TQ, w), 0)
    qk = jnp.bitwise_and(lax.broadcasted_iota(jnp.int32, (TQ, w), 1), TQ - 1) - key

    def scores(kt, slot, live, low=None, high=None):
        kt_ld = jnp.clip(kt, 0, nq - 1)
        ks = k_ref[0, 0, pl.ds(pl.multiple_of(kt_ld * TQ, TQ), TQ), :]
        s = _dot(ks, q) + bias_ref[0, jnp.clip(i - kt, 0, n_bias - 1)]
        if selected:
            per = TQ // SEL_LEN
            hit = selv_ref[pl.ds(kt_ld * per + per - 1, 1), :]
            for c in range(per - 2, -1, -1):
                hit = jnp.where(key < (c + 1) * SEL_LEN, selv_ref[pl.ds(kt_ld * per + c, 1), :], hit)
            s = jnp.where(hit > 0.5, s, NEG)
        if low is not None:
            s = jnp.where(qk >= low, s, NEG)
        if high is not None:
            s = jnp.where(qk < high, s, NEG)
        if live is not None:
            s = jnp.where(live, s, NEG)
        s_ref[slot] = s
        return _fold_rows(s, jnp.max)

    def weights(kt, slot, m):
        p = jnp.exp(s_ref[slot] - m)
        return _fold_rows(p, jnp.sum), _dot(vt_ref[0, 0, jnp.clip(kt, 0, nq - 1)], p.astype(BF16))

    mx = jnp.full((SUBLANES, w), NEG, F32)
    l8 = jnp.zeros((SUBLANES, w), F32)
    acc = jnp.zeros((d, w), F32)
    if selected:
        n_it = (i + NSA_TILES_PER_ITER - 1) // NSA_TILES_PER_ITER
        diag = s_ref.shape[0] - 1

        def pass1(it, mx):
            for u in range(NSA_TILES_PER_ITER):
                kt = it * NSA_TILES_PER_ITER + u
                mx = jnp.maximum(mx, scores(kt, kt, kt < i))
            return mx

        mx = lax.fori_loop(0, n_it, pass1, mx)
        m = jnp.max(jnp.maximum(mx, scores(i, diag, None, low=0)), axis=0, keepdims=True)

        def pass2(it, carry):
            l8, acc = carry
            for u in range(NSA_TILES_PER_ITER):
                kt = it * NSA_TILES_PER_ITER + u
                dl, da = weights(kt, kt, m)
                l8, acc = l8 + dl, acc + da
            return l8, acc

        l8, acc = lax.fori_loop(0, n_it, pass2, (l8, acc))
        dl, da = weights(i, diag, m)
        l8, acc = l8 + dl, acc + da
    else:
        n_win = s_ref.shape[0]
        tiles = [(i - (n_win - 1) + u, u) for u in range(n_win)]
        for kt, u in tiles:
            mx = jnp.maximum(mx, scores(kt, u, kt >= 0, low=0 if u == n_win - 1 else None,
                                        high=0 if u == 0 else None))
        m = jnp.max(mx, axis=0, keepdims=True)
        for kt, u in tiles:
            dl, da = weights(kt, u, m)
            l8, acc = l8 + dl, acc + da
    o_ref[0] = _heads_to_tokens(acc / jnp.sum(l8, axis=0, keepdims=True))


def _nsa_attention(proj, col_q, k, v_t, bias_t, sel=None):
    b, g, t, d = k.shape
    nq = t // TQ
    w = NSA_HPG * TQ
    n_bias = bias_t.shape[1]
    selected = sel is not None
    in_specs = [
        _q_spec(col_q),
        pl.BlockSpec((1, 1, t, d), lambda bi, gi, i: (bi, gi, 0, 0)),
        pl.BlockSpec((1, 1, nq, d, TQ), lambda bi, gi, i: (bi, gi, 0, 0, 0)),
        pl.BlockSpec((1, n_bias, TQ, w), lambda bi, gi, i: (gi, 0, 0, 0)),
    ]
    args = [proj, k, v_t, bias_t]
    scratch = [pltpu.VMEM((nq + NSA_TILES_PER_ITER if selected else WINDOW // TQ + 1, TQ, w), F32)]
    if selected:
        in_specs.append(pl.BlockSpec((1, 1, 1, LANES, TQ), lambda bi, gi, i: (bi, gi, i, 0, 0)))
        args.append(sel)
        scratch.append(pltpu.VMEM((LANES, w), F32))
    return pl.pallas_call(
        functools.partial(_nsa_attn_body, selected=selected, n_bias=n_bias),
        grid=(b, g, nq),
        in_specs=in_specs,
        out_specs=_o_spec(),
        out_shape=jax.ShapeDtypeStruct((b, t, NSA_Q), F32),
        scratch_shapes=scratch,
        compiler_params=_cparams(("parallel", "parallel", "parallel")),
        name="sel_attention" if selected else "win_attention",
    )(*args)


def _sb_body(q_ref, k_ref, v_ref, uo_ref, o_ref, c_ref):
    i = pl.program_id(2)
    uo = uo_ref[...]
    hb = c_ref.shape[0]
    pairs = range(hb // 2)
    row = lax.broadcasted_iota(jnp.int32, (TQ, TQ), 0)
    col = lax.broadcasted_iota(jnp.int32, (TQ, TQ), 1)
    rc = row - col
    first = col < HEAD_DIM
    c_ref[...] = jnp.zeros_like(c_ref)
    o_ref[...] = jnp.zeros_like(o_ref)
    qs = []
    for p in pairs:
        qp = q_ref[0, :, p * LANES:(p + 1) * LANES]
        zero = jnp.zeros_like(qp)
        qs += [jnp.where(first, qp, zero), jnp.where(first, zero, qp)]

    def cond(st):
        kt, cmax = st
        return (kt >= 0) & (cmax > SB_SKIP_LOG)

    def body(st):
        kt, _ = st
        start = pl.multiple_of(kt * TQ, TQ)
        mask = ((i - kt) * TQ + rc) > 0
        heads = range(hb)
        ks = [k_ref[0, pl.ds(start, TQ), p * LANES:(p + 1) * LANES] for p in pairs]
        vs = [v_ref[0, pl.ds(start, TQ), p * LANES:(p + 1) * LANES] for p in pairs]
        zs = [_dot_nt(qs[h], ks[h // 2]) for h in heads]
        lbs = [jnp.minimum(z, 0.0) - jnp.log(1.0 + jnp.exp(-jnp.abs(z))) for z in zs]
        parts = [_split_bf16(jnp.where(mask, lbs[h] - zs[h], 0.0)) for h in heads]
        sfxs = [_dot(hi, uo) + _dot(lo, uo) for hi, lo in parts]
        ws = [jnp.where(mask, jnp.exp(lbs[h] + sfxs[h][:, :TQ] + c_ref[h]), 0.0).astype(BF16) for h in heads]
        cm = None
        for p in pairs:
            o_ref[0, :, p * LANES:(p + 1) * LANES] += jnp.where(first, _dot(ws[2 * p], vs[p]),
                                                                _dot(ws[2 * p + 1], vs[p]))
        for h in heads:
            c = c_ref[h] + sfxs[h][:, TQ:]
            c_ref[h] = c
            cm = c if cm is None else jnp.maximum(cm, c)
        return kt - 1, jnp.max(cm)

    lax.while_loop(cond, body, (i, jnp.float32(0.0)))


def _stick_breaking(proj, col_q, col_k, col_v):
    b, t, _ = proj.shape
    hb = SB_HEADS_PER_STEP
    wb = hb * HEAD_DIM
    tri = np.triu(np.ones((TQ, TQ), np.float32), 0).T - np.eye(TQ, dtype=np.float32)
    uo = jnp.asarray(np.concatenate([tri, np.ones((TQ, TQ), np.float32)], axis=1), BF16)
    return pl.pallas_call(
        _sb_body,
        grid=(b, SB_HEADS // hb, t // TQ),
        in_specs=[
            pl.BlockSpec((1, TQ, wb), lambda bi, hi, i: (bi, i, col_q // wb + hi)),
            pl.BlockSpec((1, t, wb), lambda bi, hi, i: (bi, 0, col_k // wb + hi)),
            pl.BlockSpec((1, t, wb), lambda bi, hi, i: (bi, 0, col_v // wb + hi)),
            pl.BlockSpec((TQ, 2 * TQ), lambda bi, hi, i: (0, 0)),
        ],
        out_specs=pl.BlockSpec((1, TQ, wb), lambda bi, hi, i: (bi, i, hi)),
        out_shape=jax.ShapeDtypeStruct((b, t, SB_W), F32),
        scratch_shapes=[pltpu.VMEM((hb, TQ, TQ), F32)],
        compiler_params=_cparams(("parallel", "parallel", "parallel")),
        name="stick_breaking",
    )(proj, proj, proj, uo)


def _merge_body(oc_ref, os_ref, ow_ref, osb_ref, gbr_ref, ga_ref, gb_ref, e_ref, wn_ref, wsb_ref, o_ref):
    gate = jax.nn.sigmoid(gbr_ref[...].astype(F32))
    hi, lo = _split_bf16(gate)
    e = e_ref[...]
    gexp = _dot(hi, e) + _dot(lo, e)
    o_nsa = (gexp[:, :NSA_Q] * oc_ref[...] + gexp[:, NSA_Q:2 * NSA_Q] * os_ref[...]
             + gexp[:, 2 * NSA_Q:] * ow_ref[...])
    a = _dot(o_nsa.astype(BF16), wn_ref[...])
    bm = _dot(osb_ref[...].astype(BF16), wsb_ref[...])
    merged = jax.nn.sigmoid(ga_ref[...].astype(F32)) * a + jax.nn.sigmoid(gb_ref[...].astype(F32)) * bm
    o_ref[...] = merged.astype(o_ref.dtype)


def _merge(o_c, o_s, o_w, o_sb, proj, col_gbr, col_ga, col_gb, w_nsa, w_sb, tm=256):
    n = o_c.shape[0]
    dm = w_nsa.shape[1]
    e = np.zeros((LANES, 3 * NSA_Q), np.float32)
    for j in range(3 * NSA_HEADS):
        e[j, j * HEAD_DIM:(j + 1) * HEAD_DIM] = 1.0
    row = lambda i: (i, 0)
    full = lambda i: (0, 0)
    return pl.pallas_call(
        _merge_body,
        grid=(n // tm,),
        in_specs=[
            pl.BlockSpec((tm, NSA_Q), row), pl.BlockSpec((tm, NSA_Q), row), pl.BlockSpec((tm, NSA_Q), row),
            pl.BlockSpec((tm, SB_W), row),
            pl.BlockSpec((tm, LANES), lambda i: (i, col_gbr // LANES)),
            pl.BlockSpec((tm, dm), lambda i: (i, col_ga // dm)),
            pl.BlockSpec((tm, dm), lambda i: (i, col_gb // dm)),
            pl.BlockSpec((LANES, 3 * NSA_Q), full),
            pl.BlockSpec((NSA_Q, dm), full), pl.BlockSpec((SB_W, dm), full),
        ],
        out_specs=pl.BlockSpec((tm, dm), row),
        out_shape=jax.ShapeDtypeStruct((n, dm), BF16),
        compiler_params=_cparams(("parallel",)),
        name="gated_merge",
    )(o_c, o_s, o_w, o_sb, proj, proj, proj, jnp.asarray(e, BF16), w_nsa.astype(BF16), w_sb.astype(BF16))


def _peer_scores_body(wqt_ref, xt_ref, keys_ref, s_ref):
    qt = _dot(wqt_ref[...], xt_ref[...])
    c = keys_ref.shape[-1]
    for ch in range(keys_ref.shape[0]):
        qh, ql = _split_bf16(qt[ch * c:(ch + 1) * c])
        kh, kl = _split_bf16(keys_ref[ch])
        s_ref[ch * PEER_KEYS:(ch + 1) * PEER_KEYS, :] = _dot(kh, qh) + _dot(kh, ql) + _dot(kl, qh)


def _peer_scores(wq_t, x_t, keys, tn=512):
    rq, d = wq_t.shape
    n = x_t.shape[1]
    nch, nk, c = keys.shape
    return pl.pallas_call(
        _peer_scores_body,
        grid=(n // tn,),
        in_specs=[pl.BlockSpec((rq, d), lambda i: (0, 0)), pl.BlockSpec((d, tn), lambda i: (0, i)),
                  pl.BlockSpec((nch, nk, c), lambda i: (0, 0, 0))],
        out_specs=pl.BlockSpec((nch * nk, tn), lambda i: (0, i)),
        out_shape=jax.ShapeDtypeStruct((nch * nk, n), F32),
        compiler_params=_cparams(("parallel",)),
        name="peer_scores",
    )(wq_t, x_t, keys)


def _peer_route_body(s_ref, lim0_ref, c0_ref, rank1_ref, e1_ref):
    k = PEER_TOPK
    tn = s_ref.shape[-1]
    kidx = lax.broadcasted_iota(jnp.int32, (k, tn), 0)
    lg = int(math.log2(k))
    n_a, n_b = PEER_CAND_ROWS * k, PEER_CAND_COLS * k
    row = lax.broadcasted_iota(jnp.int32, (n_a + n_b, tn), 0)
    in_rows = row < n_a
    r0 = jnp.where(in_rows, jnp.right_shift(row, lg), jnp.bitwise_and(row - n_a, k - 1))
    r1 = jnp.where(in_rows, jnp.bitwise_and(row, k - 1), jnp.right_shift(row - n_a, lg))
    reachable = ((r0 + 1) * (r1 + 1) <= k) & (in_rows | (r0 >= PEER_CAND_ROWS))
    flat = (r0 * k + r1).astype(F32)
    for h in range(PEER_HEADS):
        s0 = s_ref[(2 * h) * PEER_KEYS:(2 * h + 1) * PEER_KEYS, :]
        s1 = s_ref[(2 * h + 1) * PEER_KEYS:(2 * h + 2) * PEER_KEYS, :]
        a, rank0 = _top_rows(s0, k)
        b, rank1 = _top_rows(s1, k)
        cand = jnp.concatenate([a[r:r + 1] + b for r in range(PEER_CAND_ROWS)]
                               + [a + b[c:c + 1] for c in range(PEER_CAND_COLS)], axis=0)
        cand = jnp.where(reachable, cand, -jnp.inf)
        best, crank = _top_rows(cand, k, order=flat)
        chosen = jnp.where(crank < float(k), 1.0, 0.0)
        e = chosen * jnp.exp(jnp.minimum(cand - best[0:1], 0.0))
        z = jnp.sum(e, axis=0, keepdims=True)
        cnt = chosen[n_a:n_a + k]
        for c in range(1, PEER_CAND_COLS):
            cnt = cnt + chosen[n_a + c * k:n_a + (c + 1) * k]
        for r in range(PEER_CAND_ROWS):
            n_r = jnp.sum(chosen[r * k:(r + 1) * k], axis=0, keepdims=True)
            cnt = jnp.where(kidx == r, n_r, cnt)
        lim0 = jnp.zeros_like(s0)
        for r in range(k):
            lim0 = jnp.where(rank0 == float(r), cnt[r:r + 1], lim0)
        lim0_ref[h] = lim0
        c0_ref[h] = jnp.exp(s0 - a[0:1]) / z
        rank1_ref[h] = rank1.astype(rank1_ref.dtype)
        e1_ref[h] = jnp.exp(s1 - b[0:1]).astype(e1_ref.dtype)


def _peer_route(s_t, tn=256):
    rows, n = s_t.shape
    shp = jax.ShapeDtypeStruct((PEER_HEADS, PEER_KEYS, n), F32)
    spec = pl.BlockSpec((PEER_HEADS, PEER_KEYS, tn), lambda i: (0, 0, i))
    return pl.pallas_call(
        _peer_route_body,
        grid=(n // tn,),
        in_specs=[pl.BlockSpec((rows, tn), lambda i: (0, i))],
        out_specs=[spec, spec, spec, spec],
        out_shape=[shp, shp, shp, shp],
        compiler_params=_cparams(("parallel",)),
        name="peer_route",
    )(s_t)


def _peer_main_body(u_ref, vt_ref, xt_ref, lim0_ref, c0_ref, rank1_in, e1_in, o_ref, act_ref, w_ref, rank1_ref,
                    e1_ref, *, te, ts):
    et = pl.program_id(1)

    @pl.when(et == 0)
    def _():
        o_ref[...] = jnp.zeros_like(o_ref)
        for h in range(PEER_HEADS):
            rank1_ref[h] = rank1_in[h].astype(BF16)
            e1_ref[h] = e1_in[h].astype(BF16)

    xt = xt_ref[...]
    tn = xt.shape[1]
    n_i = ts // PEER_KEYS
    for sb in range(te // ts):
        act_ref[sb] = _dot(u_ref[sb * ts:(sb + 1) * ts, :], xt)
    pk = 2 * SUBLANES
    grp = (PEER_KEYS // pk, pk, LANES)
    for sb in range(te // ts):
        for lc in range(tn // LANES):
            cols = slice(lc * LANES, (lc + 1) * LANES)
            ss = [jnp.zeros(grp, BF16) for _ in range(n_i)]
            for h in range(PEER_HEADS):
                r1 = rank1_ref[h, :, cols].reshape(grp)
                e1 = e1_ref[h, :, cols].reshape(grp)
                for ii in range(n_i):
                    r = sb * n_i + ii
                    lim = jnp.broadcast_to(lim0_ref[h, r:r + 1, cols], (pk, LANES)).astype(BF16)
                    cc = jnp.broadcast_to(c0_ref[h, r:r + 1, cols], (pk, LANES)).astype(BF16)
                    ss[ii] = ss[ii] + jnp.where(r1 < lim[None], e1 * cc[None], jnp.zeros((), BF16))
            for ii in range(n_i):
                rows = slice(ii * PEER_KEYS, (ii + 1) * PEER_KEYS)
                g = jax.nn.gelu(act_ref[sb, rows, cols]).astype(BF16)
                w_ref[sb, rows, cols] = ss[ii].reshape(PEER_KEYS, LANES) * g
        o_ref[...] += _dot(vt_ref[0, :, sb * ts:(sb + 1) * ts], w_ref[sb])


def _peer_main(u, v_t, x_t, lim0, c0, rank1, e1, tn=512, te=1024, ts=256):
    n_exp, d = u.shape
    n = x_t.shape[1]
    tab = pl.BlockSpec((PEER_HEADS, PEER_KEYS, tn), lambda i, j: (0, 0, i))
    tab0 = pl.BlockSpec((PEER_HEADS, te // PEER_KEYS, tn), lambda i, j: (0, j, i))
    return pl.pallas_call(
        functools.partial(_peer_main_body, te=te, ts=ts),
        grid=(n // tn, n_exp // te),
        in_specs=[pl.BlockSpec((te, d), lambda i, j: (j, 0)), pl.BlockSpec((1, d, te), lambda i, j: (j, 0, 0)),
                  pl.BlockSpec((d, tn), lambda i, j: (0, i)), tab0, tab0, tab, tab],
        out_specs=pl.BlockSpec((d, tn), lambda i, j: (0, i)),
        out_shape=jax.ShapeDtypeStruct((d, n), F32),
        scratch_shapes=[pltpu.VMEM((te // ts, ts, tn), F32), pltpu.VMEM((te // ts, ts, tn), BF16),
                        pltpu.VMEM((PEER_HEADS, PEER_KEYS, tn), BF16), pltpu.VMEM((PEER_HEADS, PEER_KEYS, tn), BF16)],
        compiler_params=_cparams(("parallel", "arbitrary"), vmem_mb=56),
        name="peer_experts",
    )(u, v_t, x_t, lim0, c0, rank1, e1)


def kernel(x, attn_norm_g, w_in, cmp_k_pe, cmp_k_w1, cmp_k_w2, cmp_v_pe, cmp_v_w1, cmp_v_w2, rel_bias_table,
           w_branch_nsa, w_branch_sb, w_out, ffn_norm_g, peer_w_q, peer_sub_keys, peer_u, peer_v, final_norm_g):
    h = x
    for l in range(attn_norm_g.shape[0]):
        h = _layer(h, attn_norm_g[l], w_in[l], cmp_k_pe[l], cmp_k_w1[l], cmp_k_w2[l], cmp_v_pe[l], cmp_v_w1[l],
                   cmp_v_w2[l], rel_bias_table, w_branch_nsa[l], w_branch_sb[l], w_out[l], ffn_norm_g[l],
                   peer_w_q[l], peer_sub_keys[l], peer_u[l], peer_v[l],
                   final_norm_g if l == attn_norm_g.shape[0] - 1 else None)
    return h


def _layer(h, attn_g, w_in, ck_pe, ck_w1, ck_w2, cv_pe, cv_w1, cv_w2, rel_table, w_br_nsa, w_br_sb, w_out,
           ffn_g, pq, psk, pu, pv, final_g):
    b, t, dm = h.shape
    n = b * t
    g, hg, d = NSA_GROUPS, NSA_HPG, HEAD_DIM
    nq = t // TQ
    x2 = h.reshape(n, dm)

    o_gbr = NSA_Q + 6 * NSA_KV
    n_gbr = 3 * NSA_HEADS
    o_qb = o_gbr + n_gbr
    o_ga = o_qb + 3 * SB_W
    tn_in = 768
    packed = w_in.shape[1]
    packed_pad = -(-packed // tn_in) * tn_in
    scale = HEAD_DIM ** -0.5
    w_pack = jnp.concatenate([w_in[:, o_ga:], w_in[:, :NSA_Q] * scale, w_in[:, NSA_Q:o_gbr],
                              w_in[:, o_qb:o_qb + SB_W] * scale, w_in[:, o_qb + SB_W:o_ga], w_in[:, o_gbr:o_qb],
                              jnp.zeros((dm, packed_pad - packed), w_in.dtype)], axis=1).astype(BF16)
    a = _rmsnorm(x2, attn_g, BF16)
    proj = _matmul(a, w_pack, 512, tn_in, BF16, name="in_proj")
    proj3 = proj.reshape(b, t, packed_pad)
    col_ga, col_gb = 0, dm
    col_qn = 2 * dm
    col_kv = col_qn + NSA_Q
    col_qb = col_kv + 6 * NSA_KV
    col_gbr = col_qb + 3 * SB_W

    def heads_kv(z):
        return z.reshape(b, t, g, d).transpose(0, 2, 1, 3)

    def tiles_kv_t(z):
        return z.reshape(b, nq, TQ, g, d).transpose(0, 3, 1, 4, 2)

    kv_cols = [proj[:, col_kv + j * NSA_KV:col_kv + (j + 1) * NSA_KV] for j in range(6)]
    kc_tok, vc_tok = heads_kv(kv_cols[0]), heads_kv(kv_cols[1])
    ks, vs_t = heads_kv(kv_cols[2]), tiles_kv_t(kv_cols[3])
    kw, vw_t = heads_kv(kv_cols[4]), tiles_kv_t(kv_cols[5])

    n_cmp = (t - CMP_LEN) // CMP_STRIDE + 1
    n_chunk = t // CMP_STRIDE
    ncp = -(-n_cmp // LANES) * LANES
    reps = CMP_LEN // CMP_STRIDE

    def blocks(tok):
        ch = tok.reshape(b, g, n_chunk, CMP_STRIDE * d)
        ch = jnp.pad(ch, ((0, 0), (0, 0), (0, ncp + reps - 1 - n_chunk), (0, 0)))
        blk = jnp.concatenate([ch[:, :, r:r + ncp] for r in range(reps)], axis=-1)
        return blk.reshape(b * g * ncp, CMP_LEN * d)

    kc_blk = _compress(blocks(kc_tok), ck_pe.reshape(1, CMP_LEN * d), ck_w1, ck_w2, ncp).reshape(b, g, ncp, d)
    vc_blk = _compress(blocks(vc_tok), cv_pe.reshape(1, CMP_LEN * d), cv_w1, cv_w2, ncp).reshape(b, g, ncp, d)

    r_i = jnp.arange(TQ, dtype=jnp.int32)
    per_tile = TQ // CMP_STRIDE
    rows_c = ncp + per_tile * (nq - 1)
    shift_c = per_tile * (nq - 1) - jnp.arange(rows_c, dtype=jnp.int32)
    dist_c = CMP_STRIDE * shift_c[None, :, None] + r_i[None, None, :] - (CMP_LEN - 1)
    bias_c = _bias_expand(rel_table, _rel_bucket(dist_c))
    n_bias = min(nq, -(-(REL_MAX_DIST + TQ - 1) // TQ) + 1)
    dist_t = jnp.arange(n_bias, dtype=jnp.int32)[:, None, None] * TQ + r_i[None, None, :] - r_i[None, :, None]
    bias_t = _bias_expand(rel_table, _rel_bucket(dist_t))

    n_sel = t // SEL_LEN
    assert n_sel <= LANES and t % TQ == 0 and t >= WINDOW + TQ, "selection table holds one block per row of a tile"
    nsp = min(LANES, -(-n_sel // 16) * 16)
    c_start = np.arange(ncp) * CMP_STRIDE
    s_start = np.arange(nsp) * SEL_LEN
    overlap_t = np.maximum(np.minimum(c_start[None, :] + CMP_LEN, s_start[:, None] + SEL_LEN)
                           - np.maximum(c_start[None, :], s_start[:, None]), 0).astype(np.float32) / CMP_LEN
    overlap_t[:, n_cmp:] = 0.0
    overlap_t[n_sel:, :] = 0.0
    o_c, sel = _cmp_attention(proj3, col_qn, kc_blk, vc_blk.transpose(0, 1, 3, 2), bias_c,
                              jnp.asarray(overlap_t, BF16), n_cmp, min(SEL_TOPK, n_sel))
    o_s = _nsa_attention(proj3, col_qn, ks, vs_t, bias_t, sel)
    o_w = _nsa_attention(proj3, col_qn, kw, vw_t, bias_t)
    o_sb = _stick_breaking(proj3, col_qb, col_qb + SB_W, col_qb + 2 * SB_W)

    merged = _merge(o_c.reshape(n, NSA_Q), o_s.reshape(n, NSA_Q), o_w.reshape(n, NSA_Q), o_sb.reshape(n, SB_W),
                    proj, col_gbr, col_ga, col_gb, w_br_nsa, w_br_sb)
    h1 = _matmul(merged, w_out.astype(BF16), 512, 1024, F32, res=x2, name="out_proj")

    xn_t = _rmsnorm(h1, ffn_g, BF16).T
    keys = psk.reshape(PEER_HEADS * 2, PEER_KEYS, -1)
    s_t = _peer_scores(pq.T.astype(BF16), xn_t, keys)
    lim0, c0, rank1, e1 = _peer_route(s_t)
    te = 1024
    pv_t = pv.reshape(-1, te, dm).transpose(0, 2, 1).astype(BF16)
    ffn_t = _peer_main(pu.astype(BF16), pv_t, xn_t, lim0, c0, rank1, e1, te=te)
    ffn = ffn_t.T
    if final_g is None:
        return (h1 + ffn).reshape(b, t, dm)
    return _add_rmsnorm(h1, ffn, final_g).reshape(b, t, dm)
```

```python
import functools
import math

import jax
import jax.numpy as jnp
import numpy as np
from jax import lax
from jax.experimental import pallas as pl
from jax.experimental.pallas import tpu as pltpu

F32 = jnp.float32
BF16 = jnp.bfloat16

HEAD_DIM = 64
NSA_HEADS = 16
NSA_GROUPS = 4
NSA_HPG = NSA_HEADS // NSA_GROUPS
SB_HEADS = 16
CMP_LEN = 32
CMP_STRIDE = 16
SEL_LEN = 64
SEL_TOPK = 16
N_LOCAL_SEL = 2
SEL_FORCE_BONUS = 1e4
WINDOW = 512
REL_BUCKETS = 32
REL_MAX_DIST = 1024
PEER_HEADS = 8
PEER_KEYS = 128
PEER_TOPK = 16
EPS = 1e-6
NEG = -1e30

NSA_Q = NSA_HEADS * HEAD_DIM
NSA_KV = NSA_GROUPS * HEAD_DIM
SB_W = SB_HEADS * HEAD_DIM

LANES = 128
SUBLANES = 8
TQ = 128
NSA_TILES_PER_ITER = 4
SB_HEADS_PER_STEP = 8
SB_SKIP_LOG = -110.0
PEER_CAND_ROWS = 4
PEER_CAND_COLS = 3
assert all(r0 < PEER_CAND_ROWS or r1 < PEER_CAND_COLS for r0 in range(PEER_TOPK) for r1 in range(PEER_TOPK)
           if (r0 + 1) * (r1 + 1) <= PEER_TOPK)

_NT = (((1,), (1,)), ((), ()))


def _cparams(sem, vmem_mb=48):
    return pltpu.CompilerParams(dimension_semantics=sem, vmem_limit_bytes=vmem_mb * 1024 * 1024)


def _dot(a, b):
    return jnp.dot(a, b, preferred_element_type=F32)


def _dot_nt(a, b):
    return lax.dot_general(a, b, _NT, preferred_element_type=F32)


def _split_bf16(x):
    hi = x.astype(BF16)
    lo = (x - hi.astype(F32)).astype(BF16)
    return hi, lo


def _fold_rows(x, op):
    r, l = x.shape
    return op(x.reshape(r // SUBLANES, SUBLANES, l), axis=0)


def _rmsnorm_body(x_ref, g_ref, o_ref):
    x = x_ref[...]
    y = x * lax.rsqrt(jnp.mean(x * x, axis=-1, keepdims=True) + EPS)
    o_ref[...] = (y * g_ref[...]).astype(o_ref.dtype)


def _rmsnorm(x, g, out_dtype, tm=512):
    n, d = x.shape
    return pl.pallas_call(
        _rmsnorm_body,
        grid=(n // tm,),
        in_specs=[pl.BlockSpec((tm, d), lambda i: (i, 0)), pl.BlockSpec((1, d), lambda i: (0, 0))],
        out_specs=pl.BlockSpec((tm, d), lambda i: (i, 0)),
        out_shape=jax.ShapeDtypeStruct((n, d), out_dtype),
        compiler_params=_cparams(("parallel",)),
        name="rmsnorm",
    )(x, g.reshape(1, d))


def _rmsnorm_t_body(x_ref, g_ref, o_ref):
    x = x_ref[...]
    y = x * lax.rsqrt(jnp.mean(x * x, axis=-1, keepdims=True) + EPS)
    o_ref[...] = (y * g_ref[...]).T.astype(o_ref.dtype)


def _rmsnorm_t(x, g, out_dtype, tm=512):
    n, d = x.shape
    return pl.pallas_call(
        _rmsnorm_t_body,
        grid=(n // tm,),
        in_specs=[pl.BlockSpec((tm, d), lambda i: (i, 0)), pl.BlockSpec((1, d), lambda i: (0, 0))],
        out_specs=pl.BlockSpec((d, tm), lambda i: (0, i)),
        out_shape=jax.ShapeDtypeStruct((d, n), out_dtype),
        compiler_params=_cparams(("parallel",)),
        name="rmsnorm_t",
    )(x, g.reshape(1, d))


def _add_rmsnorm_body(x_ref, yt_ref, g_ref, o_ref):
    x = x_ref[...] + yt_ref[...].T
    y = x * lax.rsqrt(jnp.mean(x * x, axis=-1, keepdims=True) + EPS)
    o_ref[...] = (y * g_ref[...]).astype(o_ref.dtype)


def _add_rmsnorm(x, y_t, g, tm=512):
    n, d = x.shape
    return pl.pallas_call(
        _add_rmsnorm_body,
        grid=(n // tm,),
        in_specs=[pl.BlockSpec((tm, d), lambda i: (i, 0)), pl.BlockSpec((d, tm), lambda i: (0, i)),
                  pl.BlockSpec((1, d), lambda i: (0, 0))],
        out_specs=pl.BlockSpec((tm, d), lambda i: (i, 0)),
        out_shape=jax.ShapeDtypeStruct((n, d), F32),
        compiler_params=_cparams(("parallel",)),
        name="add_rmsnorm",
    )(x, y_t, g.reshape(1, d))


def _mm_body(a_ref, w_ref, o_ref):
    o_ref[...] = _dot(a_ref[...], w_ref[...]).astype(o_ref.dtype)


def _mm_res_body(a_ref, w_ref, r_ref, o_ref):
    o_ref[...] = (r_ref[...] + _dot(a_ref[...], w_ref[...])).astype(o_ref.dtype)


def _matmul(a, w, tm, tn, out_dtype, res=None, name="matmul"):
    m, k = a.shape
    nc = w.shape[1]
    in_specs = [pl.BlockSpec((tm, k), lambda j, i: (i, 0)), pl.BlockSpec((k, tn), lambda j, i: (0, j))]
    args = [a, w]
    body = _mm_body
    if res is not None:
        in_specs.append(pl.BlockSpec((tm, tn), lambda j, i: (i, j)))
        args.append(res)
        body = _mm_res_body
    return pl.pallas_call(
        body,
        grid=(nc // tn, m // tm),
        in_specs=in_specs,
        out_specs=pl.BlockSpec((tm, tn), lambda j, i: (i, j)),
        out_shape=jax.ShapeDtypeStruct((m, nc), out_dtype),
        compiler_params=_cparams(("parallel", "parallel")),
        name=name,
    )(*args)


def _rel_bucket(dist):
    dist = jnp.maximum(dist, 0)
    n_exact = REL_BUCKETS // 2
    d_f = jnp.maximum(dist, 1).astype(jnp.float32)
    large = n_exact + (jnp.log(d_f / n_exact) / math.log(REL_MAX_DIST / n_exact)
                       * (REL_BUCKETS - n_exact)).astype(jnp.int32)
    large = jnp.minimum(large, REL_BUCKETS - 1)
    return jnp.bitwise_and(jnp.where(dist < n_exact, dist, large), REL_BUCKETS - 1)


def _bias_expand_body(tab_ref, bkt_ref, o_ref):
    h = pl.program_id(1)
    bkt = bkt_ref[0]
    acc = jnp.full(bkt.shape, tab_ref[0, h], F32)
    for k in range(1, REL_BUCKETS):
        acc = jnp.where(bkt >= k, tab_ref[k, h], acc)
    o_ref[0, 0] = acc


def _bias_expand(table, bucket):
    r, rows, _ = bucket.shape
    return pl.pallas_call(
        _bias_expand_body,
        grid=(r, NSA_HEADS),
        in_specs=[pl.BlockSpec(memory_space=pltpu.SMEM), pl.BlockSpec((1, rows, TQ), lambda i, h: (i, 0, 0))],
        out_specs=pl.BlockSpec((1, 1, rows, TQ), lambda i, h: (h // NSA_HPG, i, 0, h % NSA_HPG)),
        out_shape=jax.ShapeDtypeStruct((NSA_GROUPS, r, rows, NSA_HPG * TQ), F32),
        compiler_params=_cparams(("parallel", "parallel")),
        name="bias_expand",
    )(table, bucket)


def _top_rows(x, k, order=None):
    r, l = x.shape
    ridx = lax.broadcasted_iota(jnp.int32, (r, l), 0).astype(F32) if order is None else order
    kidx = lax.broadcasted_iota(jnp.int32, (k, l), 0)
    rank = jnp.full((r, l), float(k), F32)
    vals = jnp.zeros((k, l), F32)
    cur = x
    for j in range(k):
        m = jnp.max(cur, axis=0, keepdims=True)
        idx = jnp.min(jnp.where(cur == m, ridx, jnp.inf), axis=0, keepdims=True)
        pick = ridx == idx
        rank = jnp.where(pick, float(j), rank)
        cur = jnp.where(pick, -jnp.inf, cur)
        vals = jnp.where(kidx == j, m, vals)
    return vals, rank


def _compress_body(blk_ref, pe_ref, w1_ref, w2_ref, o_ref):
    blk = (blk_ref[...].astype(F32) + pe_ref[...]).astype(BF16)
    h = jax.nn.gelu(_dot(blk, w1_ref[...]))
    o_ref[...] = _dot(h.astype(BF16), w2_ref[...]).astype(o_ref.dtype)


def _compress(blk, pe_flat, w1, w2, tr):
    r, kd = blk.shape
    hid = w1.shape[1]
    d = w2.shape[1]
    return pl.pallas_call(
        _compress_body,
        grid=(r // tr,),
        in_specs=[pl.BlockSpec((tr, kd), lambda i: (i, 0)), pl.BlockSpec((1, kd), lambda i: (0, 0)),
                  pl.BlockSpec((kd, hid), lambda i: (0, 0)), pl.BlockSpec((hid, d), lambda i: (0, 0))],
        out_specs=pl.BlockSpec((tr, d), lambda i: (i, 0)),
        out_shape=jax.ShapeDtypeStruct((r, d), BF16),
        compiler_params=_cparams(("parallel",)),
        name="compress_mlp",
    )(blk, pe_flat, w1.astype(BF16), w2.astype(BF16))


def _heads_on_lanes(q_blk):
    qt = q_blk.astype(F32).T
    d = qt.shape[0] // NSA_HPG
    return jnp.concatenate([qt[h * d:(h + 1) * d] for h in range(NSA_HPG)], axis=1).astype(BF16)


def _heads_to_tokens(o):
    return jnp.concatenate([o[:, h * TQ:(h + 1) * TQ] for h in range(NSA_HPG)], axis=0).T


def _cmp_attn_body(q_ref, kc_ref, vct_ref, bias_ref, ovt_ref, o_ref, sel_ref, *, n_cmp, n_top):
    i = pl.program_id(2)
    nq = pl.num_programs(2)
    q = _heads_on_lanes(q_ref[0])
    kc = kc_ref[0, 0]
    ncp, w = kc.shape[0], q.shape[1]
    blk = lax.broadcasted_iota(jnp.int32, (ncp, w), 0)
    t = i * TQ + jnp.bitwise_and(lax.broadcasted_iota(jnp.int32, (ncp, w), 1), TQ - 1)
    valid = ((t - CMP_STRIDE * blk - (CMP_LEN - 1)) >= 0) & (blk < n_cmp)
    off = pl.multiple_of((nq - 1 - i) * (TQ // CMP_STRIDE), TQ // CMP_STRIDE)
    s = jnp.where(valid, _dot(kc, q) + bias_ref[0, 0, pl.ds(off, ncp), :], NEG)
    m = jnp.max(s, axis=0, keepdims=True)
    e = jnp.where(valid, jnp.exp(s - m), 0.0)
    l = jnp.sum(e, axis=0, keepdims=True)
    p = e / jnp.maximum(l, 1e-30)
    o_ref[0] = _heads_to_tokens(_dot(vct_ref[0, 0], p.astype(BF16)))
    psum = p[:, :TQ]
    for h in range(1, NSA_HPG):
        psum = psum + p[:, h * TQ:(h + 1) * TQ]
    hi, lo = _split_bf16(psum)
    ovt = ovt_ref[...]
    imp = _dot(ovt, hi) + _dot(ovt, lo)
    nsp = imp.shape[0]
    j = lax.broadcasted_iota(jnp.int32, (nsp, TQ), 0)
    cur = jnp.right_shift(i * TQ + lax.broadcasted_iota(jnp.int32, (nsp, TQ), 1), int(math.log2(SEL_LEN)))
    gap = cur - j
    forced = (j == 0) | ((gap >= 0) & (gap < N_LOCAL_SEL))
    allowed = j <= cur
    score = jnp.where(allowed, imp + SEL_FORCE_BONUS * forced.astype(F32), -jnp.inf)
    _, rank = _top_rows(score, n_top)
    sel = jnp.where((rank < float(n_top)) & allowed, 1.0, 0.0)
    if nsp < LANES:
        sel = jnp.concatenate([sel, jnp.zeros((LANES - nsp, TQ), F32)], axis=0)
    sel_ref[0, 0, 0] = sel.astype(sel_ref.dtype)


def _q_spec(col_q):
    wq = NSA_HPG * HEAD_DIM
    return pl.BlockSpec((1, TQ, wq), lambda bi, gi, i: (bi, i, col_q // wq + gi))


def _o_spec():
    return pl.BlockSpec((1, TQ, NSA_HPG * HEAD_DIM), lambda bi, gi, i: (bi, i, gi))


def _cmp_attention(proj, col_q, kc, vc_t, bias_c, overlap_t, n_cmp, n_top):
    b, t, _ = proj.shape
    g, ncp, d = kc.shape[1:]
    nq = t // TQ
    w = NSA_HPG * TQ
    return pl.pallas_call(
        functools.partial(_cmp_attn_body, n_cmp=n_cmp, n_top=n_top),
        grid=(b, g, nq),
        in_specs=[
            _q_spec(col_q),
            pl.BlockSpec((1, 1, ncp, d), lambda bi, gi, i: (bi, gi, 0, 0)),
            pl.BlockSpec((1, 1, d, ncp), lambda bi, gi, i: (bi, gi, 0, 0)),
            pl.BlockSpec((1, 1, bias_c.shape[2], w), lambda bi, gi, i: (gi, 0, 0, 0)),
            pl.BlockSpec(overlap_t.shape, lambda bi, gi, i: (0, 0)),
        ],
        out_specs=[
            _o_spec(),
            pl.BlockSpec((1, 1, 1, LANES, TQ), lambda bi, gi, i: (bi, gi, i, 0, 0)),
        ],
        out_shape=[jax.ShapeDtypeStruct((b, t, NSA_Q), F32), jax.ShapeDtypeStruct((b, g, nq, LANES, TQ), BF16)],
        compiler_params=_cparams(("parallel", "parallel", "parallel")),
        name="cmp_attention",
    )(proj, kc, vc_t, bias_c, overlap_t)


def _nsa_attn_body(q_ref, k_ref, vt_ref, bias_ref, *rest, selected, n_bias):
    if selected:
        sel_ref, o_ref, s_ref, selv_ref = rest
        selv_ref[...] = sel_ref[0, 0, 0].astype(F32)
    else:
        o_ref, s_ref = rest
    i = pl.program_id(2)
    q = _heads_on_lanes(q_ref[0])
    d, w = q.shape
    nq = vt_ref.shape[2]
    key = lax.broadcasted_iota(jnp.int32, (TQ, w), 0)
    qk = jnp.bitwise_and(lax.broadcasted_iota(jnp.int32, (TQ, w), 1), TQ - 1) - key

    def scores(kt, slot, live, low=None, high=None):
        kt_ld = jnp.clip(kt, 0, nq - 1)
        ks = k_ref[0, 0, pl.ds(pl.multiple_of(kt_ld * TQ, TQ), TQ), :]
        s = _dot(ks, q) + bias_ref[0, jnp.clip(i - kt, 0, n_bias - 1)]
        if selected:
            per = TQ // SEL_LEN
            rows = [selv_ref[pl.ds(kt_ld * per + c, 1), :] for c in range(per)]
            if live is not None:
                rows, live = [jnp.where(live, r, 0.0) for r in rows], None
            hit = rows[per - 1]
            for c in range(per - 2, -1, -1):
                hit = jnp.where(key[:, :TQ] < (c + 1) * SEL_LEN, rows[c], hit)
            ok = hit > 0.5
            s = jnp.concatenate([jnp.where(ok, s[:, h * TQ:(h + 1) * TQ], NEG) for h in range(NSA_HPG)], axis=1)
        if low is not None:
            s = jnp.where(qk >= low, s, NEG)
        if high is not None:
            s = jnp.where(qk < high, s, NEG)
        if live is not None:
            s = jnp.where(live, s, NEG)
        s_ref[slot] = s
        return _fold_rows(s, jnp.max)

    def weights(kt, slot, m):
        p = jnp.exp(s_ref[slot] - m)
        return _fold_rows(p, jnp.sum), _dot(vt_ref[0, 0, jnp.clip(kt, 0, nq - 1)], p.astype(BF16))

    mx = jnp.full((SUBLANES, w), NEG, F32)
    l8 = jnp.zeros((SUBLANES, w), F32)
    acc = jnp.zeros((d, w), F32)
    if selected:
        n_it = (i + NSA_TILES_PER_ITER - 1) // NSA_TILES_PER_ITER
        diag = s_ref.shape[0] - 1

        def pass1(it, mx):
            for u in range(NSA_TILES_PER_ITER):
                kt = it * NSA_TILES_PER_ITER + u
                mx = jnp.maximum(mx, scores(kt, kt, kt < i))
            return mx

        mx = lax.fori_loop(0, n_it, pass1, mx)
        m = jnp.max(jnp.maximum(mx, scores(i, diag, None, low=0)), axis=0, keepdims=True)

        def pass2(it, carry):
            l8, acc = carry
            for u in range(NSA_TILES_PER_ITER):
                kt = it * NSA_TILES_PER_ITER + u
                dl, da = weights(kt, kt, m)
                l8, acc = l8 + dl, acc + da
            return l8, acc

        l8, acc = lax.fori_loop(0, n_it, pass2, (l8, acc))
        dl, da = weights(i, diag, m)
        l8, acc = l8 + dl, acc + da
    else:
        n_win = s_ref.shape[0]
        tiles = [(i - (n_win - 1) + u, u) for u in range(n_win)]
        for kt, u in tiles:
            mx = jnp.maximum(mx, scores(kt, u, kt >= 0, low=0 if u == n_win - 1 else None,
                                        high=0 if u == 0 else None))
        m = jnp.max(mx, axis=0, keepdims=True)
        for kt, u in tiles:
            dl, da = weights(kt, u, m)
            l8, acc = l8 + dl, acc + da
    o_ref[0] = _heads_to_tokens(acc / jnp.sum(l8, axis=0, keepdims=True))


def _nsa_attention(proj, col_q, k, v_t, bias_t, sel=None):
    b, g, t, d = k.shape
    nq = t // TQ
    w = NSA_HPG * TQ
    n_bias = bias_t.shape[1]
    selected = sel is not None
    in_specs = [
        _q_spec(col_q),
        pl.BlockSpec((1, 1, t, d), lambda bi, gi, i: (bi, gi, 0, 0)),
        pl.BlockSpec((1, 1, nq, d, TQ), lambda bi, gi, i: (bi, gi, 0, 0, 0)),
        pl.BlockSpec((1, n_bias, TQ, w), lambda bi, gi, i: (gi, 0, 0, 0)),
    ]
    args = [proj, k, v_t, bias_t]
    scratch = [pltpu.VMEM((nq + NSA_TILES_PER_ITER if selected else WINDOW // TQ + 1, TQ, w), F32)]
    if selected:
        in_specs.append(pl.BlockSpec((1, 1, 1, LANES, TQ), lambda bi, gi, i: (bi, gi, i, 0, 0)))
        args.append(sel)
        scratch.append(pltpu.VMEM((LANES, TQ), F32))
    return pl.pallas_call(
        functools.partial(_nsa_attn_body, selected=selected, n_bias=n_bias),
        grid=(b, g, nq),
        in_specs=in_specs,
        out_specs=_o_spec(),
        out_shape=jax.ShapeDtypeStruct((b, t, NSA_Q), F32),
        scratch_shapes=scratch,
        compiler_params=_cparams(("parallel", "parallel", "parallel")),
        name="sel_attention" if selected else "win_attention",
    )(*args)


def _sb_body(q_ref, k_ref, v_ref, uo_ref, o_ref, c_ref):
    i = pl.program_id(2)
    uo = uo_ref[...]
    hb = c_ref.shape[0]
    pairs = range(hb // 2)
    row = lax.broadcasted_iota(jnp.int32, (TQ, TQ), 0)
    col = lax.broadcasted_iota(jnp.int32, (TQ, TQ), 1)
    rc = row - col
    first = col < HEAD_DIM
    c_ref[...] = jnp.zeros_like(c_ref)
    o_ref[...] = jnp.zeros_like(o_ref)
    qs = []
    for p in pairs:
        qp = q_ref[0, :, p * LANES:(p + 1) * LANES]
        zero = jnp.zeros_like(qp)
        qs += [jnp.where(first, qp, zero), jnp.where(first, zero, qp)]

    def cond(st):
        kt, cmax = st
        return (kt >= 0) & (cmax > SB_SKIP_LOG)

    def body(st):
        kt, _ = st
        start = pl.multiple_of(kt * TQ, TQ)
        mask = ((i - kt) * TQ + rc) > 0
        heads = range(hb)
        ks = [k_ref[0, pl.ds(start, TQ), p * LANES:(p + 1) * LANES] for p in pairs]
        vs = [v_ref[0, pl.ds(start, TQ), p * LANES:(p + 1) * LANES] for p in pairs]
        zs = [_dot_nt(qs[h], ks[h // 2]) for h in heads]
        lbs = [jnp.minimum(z, 0.0) - jnp.log(1.0 + jnp.exp(-jnp.abs(z))) for z in zs]
        parts = [_split_bf16(jnp.where(mask, lbs[h] - zs[h], 0.0)) for h in heads]
        sfxs = [_dot(hi, uo) + _dot(lo, uo) for hi, lo in parts]
        ws = [jnp.where(mask, jnp.exp(lbs[h] + sfxs[h][:, :TQ] + c_ref[h]), 0.0).astype(BF16) for h in heads]
        cm = None
        for p in pairs:
            o_ref[0, :, p * LANES:(p + 1) * LANES] += jnp.where(first, _dot(ws[2 * p], vs[p]),
                                                                _dot(ws[2 * p + 1], vs[p]))
        for h in heads:
            c = c_ref[h] + sfxs[h][:, TQ:]
            c_ref[h] = c
            cm = c if cm is None else jnp.maximum(cm, c)
        return kt - 1, jnp.max(cm)

    lax.while_loop(cond, body, (i, jnp.float32(0.0)))


def _stick_breaking(proj, col_q, col_k, col_v):
    b, t, _ = proj.shape
    hb = SB_HEADS_PER_STEP
    wb = hb * HEAD_DIM
    tri = np.triu(np.ones((TQ, TQ), np.float32), 0).T - np.eye(TQ, dtype=np.float32)
    uo = jnp.asarray(np.concatenate([tri, np.ones((TQ, TQ), np.float32)], axis=1), BF16)
    return pl.pallas_call(
        _sb_body,
        grid=(b, SB_HEADS // hb, t // TQ),
        in_specs=[
            pl.BlockSpec((1, TQ, wb), lambda bi, hi, i: (bi, i, col_q // wb + hi)),
            pl.BlockSpec((1, t, wb), lambda bi, hi, i: (bi, 0, col_k // wb + hi)),
            pl.BlockSpec((1, t, wb), lambda bi, hi, i: (bi, 0, col_v // wb + hi)),
            pl.BlockSpec((TQ, 2 * TQ), lambda bi, hi, i: (0, 0)),
        ],
        out_specs=pl.BlockSpec((1, TQ, wb), lambda bi, hi, i: (bi, i, hi)),
        out_shape=jax.ShapeDtypeStruct((b, t, SB_W), F32),
        scratch_shapes=[pltpu.VMEM((hb, TQ, TQ), F32)],
        compiler_params=_cparams(("parallel", "parallel", "parallel")),
        name="stick_breaking",
    )(proj, proj, proj, uo)


def _merge_body(oc_ref, os_ref, ow_ref, osb_ref, gbr_ref, ga_ref, gb_ref, e_ref, wn_ref, wsb_ref, o_ref):
    gate = jax.nn.sigmoid(gbr_ref[...].astype(F32))
    hi, lo = _split_bf16(gate)
    e = e_ref[...]
    gexp = _dot(hi, e) + _dot(lo, e)
    o_nsa = (gexp[:, :NSA_Q] * oc_ref[...] + gexp[:, NSA_Q:2 * NSA_Q] * os_ref[...]
             + gexp[:, 2 * NSA_Q:] * ow_ref[...])
    a = _dot(o_nsa.astype(BF16), wn_ref[...])
    bm = _dot(osb_ref[...].astype(BF16), wsb_ref[...])
    merged = jax.nn.sigmoid(ga_ref[...].astype(F32)) * a + jax.nn.sigmoid(gb_ref[...].astype(F32)) * bm
    o_ref[...] = merged.astype(o_ref.dtype)


def _merge(o_c, o_s, o_w, o_sb, proj, col_gbr, col_ga, col_gb, w_nsa, w_sb, tm=256):
    n = o_c.shape[0]
    dm = w_nsa.shape[1]
    e = np.zeros((LANES, 3 * NSA_Q), np.float32)
    for j in range(3 * NSA_HEADS):
        e[j, j * HEAD_DIM:(j + 1) * HEAD_DIM] = 1.0
    row = lambda i: (i, 0)
    full = lambda i: (0, 0)
    return pl.pallas_call(
        _merge_body,
        grid=(n // tm,),
        in_specs=[
            pl.BlockSpec((tm, NSA_Q), row), pl.BlockSpec((tm, NSA_Q), row), pl.BlockSpec((tm, NSA_Q), row),
            pl.BlockSpec((tm, SB_W), row),
            pl.BlockSpec((tm, LANES), lambda i: (i, col_gbr // LANES)),
            pl.BlockSpec((tm, dm), lambda i: (i, col_ga // dm)),
            pl.BlockSpec((tm, dm), lambda i: (i, col_gb // dm)),
            pl.BlockSpec((LANES, 3 * NSA_Q), full),
            pl.BlockSpec((NSA_Q, dm), full), pl.BlockSpec((SB_W, dm), full),
        ],
        out_specs=pl.BlockSpec((tm, dm), row),
        out_shape=jax.ShapeDtypeStruct((n, dm), BF16),
        compiler_params=_cparams(("parallel",)),
        name="gated_merge",
    )(o_c, o_s, o_w, o_sb, proj, proj, proj, jnp.asarray(e, BF16), w_nsa.astype(BF16), w_sb.astype(BF16))


def _peer_scores_body(wqt_ref, xt_ref, keys_ref, s_ref):
    qt = _dot(wqt_ref[...], xt_ref[...])
    c = keys_ref.shape[-1]
    for ch in range(keys_ref.shape[0]):
        qh, ql = _split_bf16(qt[ch * c:(ch + 1) * c])
        kh, kl = _split_bf16(keys_ref[ch])
        s_ref[ch * PEER_KEYS:(ch + 1) * PEER_KEYS, :] = _dot(kh, qh) + _dot(kh, ql) + _dot(kl, qh)


def _peer_scores(wq_t, x_t, keys, tn=512):
    rq, d = wq_t.shape
    n = x_t.shape[1]
    nch, nk, c = keys.shape
    return pl.pallas_call(
        _peer_scores_body,
        grid=(n // tn,),
        in_specs=[pl.BlockSpec((rq, d), lambda i: (0, 0)), pl.BlockSpec((d, tn), lambda i: (0, i)),
                  pl.BlockSpec((nch, nk, c), lambda i: (0, 0, 0))],
        out_specs=pl.BlockSpec((nch * nk, tn), lambda i: (0, i)),
        out_shape=jax.ShapeDtypeStruct((nch * nk, n), F32),
        compiler_params=_cparams(("parallel",)),
        name="peer_scores",
    )(wq_t, x_t, keys)


def _peer_route_body(s_ref, lim0_ref, c0_ref, rank1_ref, e1_ref):
    k = PEER_TOPK
    tn = s_ref.shape[-1]
    kidx = lax.broadcasted_iota(jnp.int32, (k, tn), 0)
    lg = int(math.log2(k))
    n_a, n_b = PEER_CAND_ROWS * k, PEER_CAND_COLS * k
    row = lax.broadcasted_iota(jnp.int32, (n_a + n_b, tn), 0)
    in_rows = row < n_a
    r0 = jnp.where(in_rows, jnp.right_shift(row, lg), jnp.bitwise_and(row - n_a, k - 1))
    r1 = jnp.where(in_rows, jnp.bitwise_and(row, k - 1), jnp.right_shift(row - n_a, lg))
    reachable = ((r0 + 1) * (r1 + 1) <= k) & (in_rows | (r0 >= PEER_CAND_ROWS))
    flat = (r0 * k + r1).astype(F32)
    for h in range(PEER_HEADS):
        s0 = s_ref[(2 * h) * PEER_KEYS:(2 * h + 1) * PEER_KEYS, :]
        s1 = s_ref[(2 * h + 1) * PEER_KEYS:(2 * h + 2) * PEER_KEYS, :]
        a, rank0 = _top_rows(s0, k)
        b, rank1 = _top_rows(s1, k)
        cand = jnp.concatenate([a[r:r + 1] + b for r in range(PEER_CAND_ROWS)]
                               + [a + b[c:c + 1] for c in range(PEER_CAND_COLS)], axis=0)
        cand = jnp.where(reachable, cand, -jnp.inf)
        best, crank = _top_rows(cand, k, order=flat)
        chosen = jnp.where(crank < float(k), 1.0, 0.0)
        e = chosen * jnp.exp(jnp.minimum(cand - best[0:1], 0.0))
        z = jnp.sum(e, axis=0, keepdims=True)
        cnt = chosen[n_a:n_a + k]
        for c in range(1, PEER_CAND_COLS):
            cnt = cnt + chosen[n_a + c * k:n_a + (c + 1) * k]
        for r in range(PEER_CAND_ROWS):
            n_r = jnp.sum(chosen[r * k:(r + 1) * k], axis=0, keepdims=True)
            cnt = jnp.where(kidx == r, n_r, cnt)
        lim0 = jnp.zeros_like(s0)
        for r in range(k):
            lim0 = jnp.where(rank0 == float(r), cnt[r:r + 1], lim0)
        lim0_ref[h] = lim0
        c0_ref[h] = jnp.exp(s0 - a[0:1]) / z
        rank1_ref[h] = rank1.astype(rank1_ref.dtype)
        e1_ref[h] = jnp.exp(s1 - b[0:1]).astype(e1_ref.dtype)


def _peer_route(s_t, tn=256):
    rows, n = s_t.shape
    shp = jax.ShapeDtypeStruct((PEER_HEADS, PEER_KEYS, n), F32)
    spec = pl.BlockSpec((PEER_HEADS, PEER_KEYS, tn), lambda i: (0, 0, i))
    return pl.pallas_call(
        _peer_route_body,
        grid=(n // tn,),
        in_specs=[pl.BlockSpec((rows, tn), lambda i: (0, i))],
        out_specs=[spec, spec, spec, spec],
        out_shape=[shp, shp, shp, shp],
        compiler_params=_cparams(("parallel",)),
        name="peer_route",
    )(s_t)


def _peer_main_body(u_ref, vt_ref, xt_ref, lim0_ref, c0_ref, rank1_in, e1_in, o_ref, act_ref, w_ref, rank1_ref,
                    e1_ref, *, te, ts):
    et = pl.program_id(1)

    @pl.when(et == 0)
    def _():
        o_ref[...] = jnp.zeros_like(o_ref)
        for h in range(PEER_HEADS):
            rank1_ref[h] = rank1_in[h].astype(BF16)
            e1_ref[h] = e1_in[h].astype(BF16)

    xt = xt_ref[...]
    tn = xt.shape[1]
    n_i = ts // PEER_KEYS
    for sb in range(te // ts):
        act_ref[sb] = _dot(u_ref[sb * ts:(sb + 1) * ts, :], xt)
    pk = 2 * SUBLANES
    grp = (PEER_KEYS // pk, pk, LANES)
    for sb in range(te // ts):
        for lc in range(tn // LANES):
            cols = slice(lc * LANES, (lc + 1) * LANES)
            ss = [jnp.zeros(grp, BF16) for _ in range(n_i)]
            for h in range(PEER_HEADS):
                r1 = rank1_ref[h, :, cols].reshape(grp)
                e1 = e1_ref[h, :, cols].reshape(grp)
                for ii in range(n_i):
                    r = sb * n_i + ii
                    lim = jnp.broadcast_to(lim0_ref[h, r:r + 1, cols], (pk, LANES)).astype(BF16)
                    cc = jnp.broadcast_to(c0_ref[h, r:r + 1, cols], (pk, LANES)).astype(BF16)
                    ss[ii] = ss[ii] + jnp.where(r1 < lim[None], e1 * cc[None], jnp.zeros((), BF16))
            for ii in range(n_i):
                rows = slice(ii * PEER_KEYS, (ii + 1) * PEER_KEYS)
                g = jax.nn.gelu(act_ref[sb, rows, cols]).astype(BF16)
                w_ref[sb, rows, cols] = ss[ii].reshape(PEER_KEYS, LANES) * g
        o_ref[...] += _dot(vt_ref[0, :, sb * ts:(sb + 1) * ts], w_ref[sb])


def _peer_main(u, v_t, x_t, lim0, c0, rank1, e1, tn=512, te=1024, ts=512):
    n_exp, d = u.shape
    n = x_t.shape[1]
    tab = pl.BlockSpec((PEER_HEADS, PEER_KEYS, tn), lambda i, j: (0, 0, i))
    tab0 = pl.BlockSpec((PEER_HEADS, te // PEER_KEYS, tn), lambda i, j: (0, j, i))
    return pl.pallas_call(
        functools.partial(_peer_main_body, te=te, ts=ts),
        grid=(n // tn, n_exp // te),
        in_specs=[pl.BlockSpec((te, d), lambda i, j: (j, 0)), pl.BlockSpec((1, d, te), lambda i, j: (j, 0, 0)),
                  pl.BlockSpec((d, tn), lambda i, j: (0, i)), tab0, tab0, tab, tab],
        out_specs=pl.BlockSpec((d, tn), lambda i, j: (0, i)),
        out_shape=jax.ShapeDtypeStruct((d, n), F32),
        scratch_shapes=[pltpu.VMEM((te // ts, ts, tn), F32), pltpu.VMEM((te // ts, ts, tn), BF16),
                        pltpu.VMEM((PEER_HEADS, PEER_KEYS, tn), BF16), pltpu.VMEM((PEER_HEADS, PEER_KEYS, tn), BF16)],
        compiler_params=_cparams(("parallel", "arbitrary"), vmem_mb=56),
        name="peer_experts",
    )(u, v_t, x_t, lim0, c0, rank1, e1)


def kernel(x, attn_norm_g, w_in, cmp_k_pe, cmp_k_w1, cmp_k_w2, cmp_v_pe, cmp_v_w1, cmp_v_w2, rel_bias_table,
           w_branch_nsa, w_branch_sb, w_out, ffn_norm_g, peer_w_q, peer_sub_keys, peer_u, peer_v, final_norm_g):
    h = x
    for l in range(attn_norm_g.shape[0]):
        h = _layer(h, attn_norm_g[l], w_in[l], cmp_k_pe[l], cmp_k_w1[l], cmp_k_w2[l], cmp_v_pe[l], cmp_v_w1[l],
                   cmp_v_w2[l], rel_bias_table, w_branch_nsa[l], w_branch_sb[l], w_out[l], ffn_norm_g[l],
                   peer_w_q[l], peer_sub_keys[l], peer_u[l], peer_v[l],
                   final_norm_g if l == attn_norm_g.shape[0] - 1 else None)
    return h


def _layer(h, attn_g, w_in, ck_pe, ck_w1, ck_w2, cv_pe, cv_w1, cv_w2, rel_table, w_br_nsa, w_br_sb, w_out,
           ffn_g, pq, psk, pu, pv, final_g):
    b, t, dm = h.shape
    n = b * t
    g, hg, d = NSA_GROUPS, NSA_HPG, HEAD_DIM
    nq = t // TQ
    x2 = h.reshape(n, dm)

    o_gbr = NSA_Q + 6 * NSA_KV
    n_gbr = 3 * NSA_HEADS
    o_qb = o_gbr + n_gbr
    o_ga = o_qb + 3 * SB_W
    tn_in = 768
    packed = w_in.shape[1]
    packed_pad = -(-packed // tn_in) * tn_in
    scale = HEAD_DIM ** -0.5
    w_pack = jnp.concatenate([w_in[:, o_ga:], w_in[:, :NSA_Q] * scale, w_in[:, NSA_Q:o_gbr],
                              w_in[:, o_qb:o_qb + SB_W] * scale, w_in[:, o_qb + SB_W:o_ga], w_in[:, o_gbr:o_qb],
                              jnp.zeros((dm, packed_pad - packed), w_in.dtype)], axis=1).astype(BF16)
    a = _rmsnorm(x2, attn_g, BF16)
    proj = _matmul(a, w_pack, 512, tn_in, BF16, name="in_proj")
    proj3 = proj.reshape(b, t, packed_pad)
    col_ga, col_gb = 0, dm
    col_qn = 2 * dm
    col_kv = col_qn + NSA_Q
    col_qb = col_kv + 6 * NSA_KV
    col_gbr = col_qb + 3 * SB_W

    def heads_kv(z):
        return z.reshape(b, t, g, d).transpose(0, 2, 1, 3)

    def tiles_kv_t(z):
        return z.reshape(b, nq, TQ, g, d).transpose(0, 3, 1, 4, 2)

    kv_cols = [proj[:, col_kv + j * NSA_KV:col_kv + (j + 1) * NSA_KV] for j in range(6)]
    kc_tok, vc_tok = heads_kv(kv_cols[0]), heads_kv(kv_cols[1])
    ks, vs_t = heads_kv(kv_cols[2]), tiles_kv_t(kv_cols[3])
    kw, vw_t = heads_kv(kv_cols[4]), tiles_kv_t(kv_cols[5])

    n_cmp = (t - CMP_LEN) // CMP_STRIDE + 1
    n_chunk = t // CMP_STRIDE
    ncp = -(-n_cmp // LANES) * LANES
    reps = CMP_LEN // CMP_STRIDE

    def blocks(tok):
        ch = tok.reshape(b, g, n_chunk, CMP_STRIDE * d)
        ch = jnp.pad(ch, ((0, 0), (0, 0), (0, ncp + reps - 1 - n_chunk), (0, 0)))
        blk = jnp.concatenate([ch[:, :, r:r + ncp] for r in range(reps)], axis=-1)
        return blk.reshape(b * g * ncp, CMP_LEN * d)

    kc_blk = _compress(blocks(kc_tok), ck_pe.reshape(1, CMP_LEN * d), ck_w1, ck_w2, ncp).reshape(b, g, ncp, d)
    vc_blk = _compress(blocks(vc_tok), cv_pe.reshape(1, CMP_LEN * d), cv_w1, cv_w2, ncp).reshape(b, g, ncp, d)

    r_i = jnp.arange(TQ, dtype=jnp.int32)
    per_tile = TQ // CMP_STRIDE
    rows_c = ncp + per_tile * (nq - 1)
    shift_c = per_tile * (nq - 1) - jnp.arange(rows_c, dtype=jnp.int32)
    dist_c = CMP_STRIDE * shift_c[None, :, None] + r_i[None, None, :] - (CMP_LEN - 1)
    bias_c = _bias_expand(rel_table, _rel_bucket(dist_c))
    n_bias = min(nq, -(-(REL_MAX_DIST + TQ - 1) // TQ) + 1)
    dist_t = jnp.arange(n_bias, dtype=jnp.int32)[:, None, None] * TQ + r_i[None, None, :] - r_i[None, :, None]
    bias_t = _bias_expand(rel_table, _rel_bucket(dist_t))

    n_sel = t // SEL_LEN
    assert n_sel <= LANES and t % TQ == 0 and t >= WINDOW + TQ, "selection table holds one block per row of a tile"
    nsp = min(LANES, -(-n_sel // 16) * 16)
    c_start = np.arange(ncp) * CMP_STRIDE
    s_start = np.arange(nsp) * SEL_LEN
    overlap_t = np.maximum(np.minimum(c_start[None, :] + CMP_LEN, s_start[:, None] + SEL_LEN)
                           - np.maximum(c_start[None, :], s_start[:, None]), 0).astype(np.float32) / CMP_LEN
    overlap_t[:, n_cmp:] = 0.0
    overlap_t[n_sel:, :] = 0.0
    o_c, sel = _cmp_attention(proj3, col_qn, kc_blk, vc_blk.transpose(0, 1, 3, 2), bias_c,
                              jnp.asarray(overlap_t, BF16), n_cmp, min(SEL_TOPK, n_sel))
    o_s = _nsa_attention(proj3, col_qn, ks, vs_t, bias_t, sel)
    o_w = _nsa_attention(proj3, col_qn, kw, vw_t, bias_t)
    o_sb = _stick_breaking(proj3, col_qb, col_qb + SB_W, col_qb + 2 * SB_W)

    merged = _merge(o_c.reshape(n, NSA_Q), o_s.reshape(n, NSA_Q), o_w.reshape(n, NSA_Q), o_sb.reshape(n, SB_W),
                    proj, col_gbr, col_ga, col_gb, w_br_nsa, w_br_sb)
    h1 = _matmul(merged, w_out.astype(BF16), 512, 1024, F32, res=x2, name="out_proj")

    xn_t = _rmsnorm_t(h1, ffn_g, BF16)
    keys = psk.reshape(PEER_HEADS * 2, PEER_KEYS, -1)
    s_t = _peer_scores(pq.T.astype(BF16), xn_t, keys)
    lim0, c0, rank1, e1 = _peer_route(s_t)
    te = 1024
    pv_t = pv.reshape(-1, te, dm).transpose(0, 2, 1).astype(BF16)
    ffn_t = _peer_main(pu.astype(BF16), pv_t, xn_t, lim0, c0, rank1, e1, te=te)
    if final_g is None:
        return (h1 + ffn_t.T).reshape(b, t, dm)
    return _add_rmsnorm(h1, ffn_t, final_g).reshape(b, t, dm)
```

```python
import functools
import math

import jax
import jax.numpy as jnp
import numpy as np
from jax import lax
from jax.experimental import pallas as pl
from jax.experimental.pallas import tpu as pltpu

F32 = jnp.float32
BF16 = jnp.bfloat16

HEAD_DIM = 64
NSA_HEADS = 16
NSA_GROUPS = 4
NSA_HPG = NSA_HEADS // NSA_GROUPS
SB_HEADS = 16
CMP_LEN = 32
CMP_STRIDE = 16
SEL_LEN = 64
SEL_TOPK = 16
N_LOCAL_SEL = 2
SEL_FORCE_BONUS = 1e4
WINDOW = 512
REL_BUCKETS = 32
REL_MAX_DIST = 1024
PEER_HEADS = 8
PEER_KEYS = 128
PEER_TOPK = 16
EPS = 1e-6
NEG = -1e30

NSA_Q = NSA_HEADS * HEAD_DIM
NSA_KV = NSA_GROUPS * HEAD_DIM
SB_W = SB_HEADS * HEAD_DIM

LANES = 128
SUBLANES = 8
TQ = 128
NSA_TILES_PER_ITER = 4
SB_HEADS_PER_STEP = 16
SB_SKIP_LOG = -110.0
PEER_CAND_ROWS = 4
PEER_CAND_COLS = 3
assert all(r0 < PEER_CAND_ROWS or r1 < PEER_CAND_COLS for r0 in range(PEER_TOPK) for r1 in range(PEER_TOPK)
           if (r0 + 1) * (r1 + 1) <= PEER_TOPK)

_NT = (((1,), (1,)), ((), ()))


def _cparams(sem, vmem_mb=48):
    return pltpu.CompilerParams(dimension_semantics=sem, vmem_limit_bytes=vmem_mb * 1024 * 1024)


def _dot(a, b):
    return jnp.dot(a, b, preferred_element_type=F32)


def _dot_nt(a, b):
    return lax.dot_general(a, b, _NT, preferred_element_type=F32)


def _split_bf16(x):
    hi = x.astype(BF16)
    lo = (x - hi.astype(F32)).astype(BF16)
    return hi, lo


def _fold_rows(x, op):
    r, l = x.shape
    return op(x.reshape(r // SUBLANES, SUBLANES, l), axis=0)


def _rmsnorm_body(x_ref, g_ref, o_ref):
    x = x_ref[...]
    y = x * lax.rsqrt(jnp.mean(x * x, axis=-1, keepdims=True) + EPS)
    o_ref[...] = (y * g_ref[...]).astype(o_ref.dtype)


def _rmsnorm(x, g, out_dtype, tm=512):
    n, d = x.shape
    return pl.pallas_call(
        _rmsnorm_body,
        grid=(n // tm,),
        in_specs=[pl.BlockSpec((tm, d), lambda i: (i, 0)), pl.BlockSpec((1, d), lambda i: (0, 0))],
        out_specs=pl.BlockSpec((tm, d), lambda i: (i, 0)),
        out_shape=jax.ShapeDtypeStruct((n, d), out_dtype),
        compiler_params=_cparams(("parallel",)),
        name="rmsnorm",
    )(x, g.reshape(1, d))


def _rmsnorm_t_body(x_ref, g_ref, o_ref):
    x = x_ref[...]
    y = x * lax.rsqrt(jnp.mean(x * x, axis=-1, keepdims=True) + EPS)
    o_ref[...] = (y * g_ref[...]).T.astype(o_ref.dtype)


def _rmsnorm_t(x, g, out_dtype, tm=512):
    n, d = x.shape
    return pl.pallas_call(
        _rmsnorm_t_body,
        grid=(n // tm,),
        in_specs=[pl.BlockSpec((tm, d), lambda i: (i, 0)), pl.BlockSpec((1, d), lambda i: (0, 0))],
        out_specs=pl.BlockSpec((d, tm), lambda i: (0, i)),
        out_shape=jax.ShapeDtypeStruct((d, n), out_dtype),
        compiler_params=_cparams(("parallel",)),
        name="rmsnorm_t",
    )(x, g.reshape(1, d))


def _add_rmsnorm_body(x_ref, yt_ref, g_ref, o_ref):
    x = x_ref[...] + yt_ref[...].T
    y = x * lax.rsqrt(jnp.mean(x * x, axis=-1, keepdims=True) + EPS)
    o_ref[...] = (y * g_ref[...]).astype(o_ref.dtype)


def _add_rmsnorm(x, y_t, g, tm=512):
    n, d = x.shape
    return pl.pallas_call(
        _add_rmsnorm_body,
        grid=(n // tm,),
        in_specs=[pl.BlockSpec((tm, d), lambda i: (i, 0)), pl.BlockSpec((d, tm), lambda i: (0, i)),
                  pl.BlockSpec((1, d), lambda i: (0, 0))],
        out_specs=pl.BlockSpec((tm, d), lambda i: (i, 0)),
        out_shape=jax.ShapeDtypeStruct((n, d), F32),
        compiler_params=_cparams(("parallel",)),
        name="add_rmsnorm",
    )(x, y_t, g.reshape(1, d))


def _mm_body(a_ref, w_ref, o_ref):
    o_ref[...] = _dot(a_ref[...], w_ref[...]).astype(o_ref.dtype)


def _mm_res_body(a_ref, w_ref, r_ref, o_ref):
    o_ref[...] = (r_ref[...] + _dot(a_ref[...], w_ref[...])).astype(o_ref.dtype)


def _matmul(a, w, tm, tn, out_dtype, res=None, name="matmul"):
    m, k = a.shape
    nc = w.shape[1]
    in_specs = [pl.BlockSpec((tm, k), lambda j, i: (i, 0)), pl.BlockSpec((k, tn), lambda j, i: (0, j))]
    args = [a, w]
    body = _mm_body
    if res is not None:
        in_specs.append(pl.BlockSpec((tm, tn), lambda j, i: (i, j)))
        args.append(res)
        body = _mm_res_body
    return pl.pallas_call(
        body,
        grid=(nc // tn, m // tm),
        in_specs=in_specs,
        out_specs=pl.BlockSpec((tm, tn), lambda j, i: (i, j)),
        out_shape=jax.ShapeDtypeStruct((m, nc), out_dtype),
        compiler_params=_cparams(("parallel", "parallel")),
        name=name,
    )(*args)


def _rel_bucket(dist):
    dist = jnp.maximum(dist, 0)
    n_exact = REL_BUCKETS // 2
    d_f = jnp.maximum(dist, 1).astype(jnp.float32)
    large = n_exact + (jnp.log(d_f / n_exact) / math.log(REL_MAX_DIST / n_exact)
                       * (REL_BUCKETS - n_exact)).astype(jnp.int32)
    large = jnp.minimum(large, REL_BUCKETS - 1)
    return jnp.bitwise_and(jnp.where(dist < n_exact, dist, large), REL_BUCKETS - 1)


def _bias_expand_body(tab_ref, bkt_ref, o_ref):
    h = pl.program_id(1)
    bkt = bkt_ref[0]
    acc = jnp.full(bkt.shape, tab_ref[0, h], F32)
    for k in range(1, REL_BUCKETS):
        acc = jnp.where(bkt >= k, tab_ref[k, h], acc)
    o_ref[0, 0] = acc


def _bias_expand(table, bucket):
    r, rows, _ = bucket.shape
    return pl.pallas_call(
        _bias_expand_body,
        grid=(r, NSA_HEADS),
        in_specs=[pl.BlockSpec(memory_space=pltpu.SMEM), pl.BlockSpec((1, rows, TQ), lambda i, h: (i, 0, 0))],
        out_specs=pl.BlockSpec((1, 1, rows, TQ), lambda i, h: (h // NSA_HPG, i, 0, h % NSA_HPG)),
        out_shape=jax.ShapeDtypeStruct((NSA_GROUPS, r, rows, NSA_HPG * TQ), F32),
        compiler_params=_cparams(("parallel", "parallel")),
        name="bias_expand",
    )(table, bucket)


def _top_rows(x, k, order=None):
    r, l = x.shape
    ridx = lax.broadcasted_iota(jnp.int32, (r, l), 0).astype(F32) if order is None else order
    kidx = lax.broadcasted_iota(jnp.int32, (k, l), 0)
    rank = jnp.full((r, l), float(k), F32)
    vals = jnp.zeros((k, l), F32)
    cur = x
    for j in range(k):
        m = jnp.max(cur, axis=0, keepdims=True)
        idx = jnp.min(jnp.where(cur == m, ridx, jnp.inf), axis=0, keepdims=True)
        pick = ridx == idx
        rank = jnp.where(pick, float(j), rank)
        cur = jnp.where(pick, -jnp.inf, cur)
        vals = jnp.where(kidx == j, m, vals)
    return vals, rank


def _compress_body(blk_ref, pe_ref, w1_ref, w2_ref, o_ref):
    blk = (blk_ref[...].astype(F32) + pe_ref[...]).astype(BF16)
    h = jax.nn.gelu(_dot(blk, w1_ref[...]))
    o_ref[...] = _dot(h.astype(BF16), w2_ref[...]).astype(o_ref.dtype)


def _compress(blk, pe_flat, w1, w2, tr):
    r, kd = blk.shape
    hid = w1.shape[1]
    d = w2.shape[1]
    return pl.pallas_call(
        _compress_body,
        grid=(r // tr,),
        in_specs=[pl.BlockSpec((tr, kd), lambda i: (i, 0)), pl.BlockSpec((1, kd), lambda i: (0, 0)),
                  pl.BlockSpec((kd, hid), lambda i: (0, 0)), pl.BlockSpec((hid, d), lambda i: (0, 0))],
        out_specs=pl.BlockSpec((tr, d), lambda i: (i, 0)),
        out_shape=jax.ShapeDtypeStruct((r, d), BF16),
        compiler_params=_cparams(("parallel",)),
        name="compress_mlp",
    )(blk, pe_flat, w1.astype(BF16), w2.astype(BF16))


def _heads_on_lanes(q_blk):
    qt = q_blk.astype(F32).T
    d = qt.shape[0] // NSA_HPG
    return jnp.concatenate([qt[h * d:(h + 1) * d] for h in range(NSA_HPG)], axis=1).astype(BF16)


def _heads_to_tokens(o):
    return jnp.concatenate([o[:, h * TQ:(h + 1) * TQ] for h in range(NSA_HPG)], axis=0).T


def _cmp_attn_body(q_ref, kc_ref, vct_ref, bias_ref, ovt_ref, o_ref, sel_ref, *, n_cmp, n_top):
    i = pl.program_id(2)
    nq = pl.num_programs(2)
    q = _heads_on_lanes(q_ref[0])
    kc = kc_ref[0, 0]
    ncp, w = kc.shape[0], q.shape[1]
    blk = lax.broadcasted_iota(jnp.int32, (ncp, w), 0)
    t = i * TQ + jnp.bitwise_and(lax.broadcasted_iota(jnp.int32, (ncp, w), 1), TQ - 1)
    valid = ((t - CMP_STRIDE * blk - (CMP_LEN - 1)) >= 0) & (blk < n_cmp)
    off = pl.multiple_of((nq - 1 - i) * (TQ // CMP_STRIDE), TQ // CMP_STRIDE)
    s = jnp.where(valid, _dot(kc, q) + bias_ref[0, 0, pl.ds(off, ncp), :], NEG)
    m = jnp.max(s, axis=0, keepdims=True)
    e = jnp.where(valid, jnp.exp(s - m), 0.0)
    l = jnp.sum(e, axis=0, keepdims=True)
    p = e / jnp.maximum(l, 1e-30)
    o_ref[0] = _heads_to_tokens(_dot(vct_ref[0, 0], p.astype(BF16)))
    psum = p[:, :TQ]
    for h in range(1, NSA_HPG):
        psum = psum + p[:, h * TQ:(h + 1) * TQ]
    hi, lo = _split_bf16(psum)
    ovt = ovt_ref[...]
    imp = _dot(ovt, hi) + _dot(ovt, lo)
    nsp = imp.shape[0]
    j = lax.broadcasted_iota(jnp.int32, (nsp, TQ), 0)
    cur = jnp.right_shift(i * TQ + lax.broadcasted_iota(jnp.int32, (nsp, TQ), 1), int(math.log2(SEL_LEN)))
    gap = cur - j
    forced = (j == 0) | ((gap >= 0) & (gap < N_LOCAL_SEL))
    allowed = j <= cur
    score = jnp.where(allowed, imp + SEL_FORCE_BONUS * forced.astype(F32), -jnp.inf)
    _, rank = _top_rows(score, n_top)
    sel = jnp.where((rank < float(n_top)) & allowed, 1.0, 0.0)
    if nsp < LANES:
        sel = jnp.concatenate([sel, jnp.zeros((LANES - nsp, TQ), F32)], axis=0)
    sel_ref[0, 0, 0] = sel.astype(sel_ref.dtype)


def _q_spec(col_q):
    wq = NSA_HPG * HEAD_DIM
    return pl.BlockSpec((1, TQ, wq), lambda bi, gi, i: (bi, i, col_q // wq + gi))


def _o_spec():
    return pl.BlockSpec((1, TQ, NSA_HPG * HEAD_DIM), lambda bi, gi, i: (bi, i, gi))


def _cmp_attention(proj, col_q, kc, vc_t, bias_c, overlap_t, n_cmp, n_top):
    b, t, _ = proj.shape
    g, ncp, d = kc.shape[1:]
    nq = t // TQ
    w = NSA_HPG * TQ
    return pl.pallas_call(
        functools.partial(_cmp_attn_body, n_cmp=n_cmp, n_top=n_top),
        grid=(b, g, nq),
        in_specs=[
            _q_spec(col_q),
            pl.BlockSpec((1, 1, ncp, d), lambda bi, gi, i: (bi, gi, 0, 0)),
            pl.BlockSpec((1, 1, d, ncp), lambda bi, gi, i: (bi, gi, 0, 0)),
            pl.BlockSpec((1, 1, bias_c.shape[2], w), lambda bi, gi, i: (gi, 0, 0, 0)),
            pl.BlockSpec(overlap_t.shape, lambda bi, gi, i: (0, 0)),
        ],
        out_specs=[
            _o_spec(),
            pl.BlockSpec((1, 1, 1, LANES, TQ), lambda bi, gi, i: (bi, gi, i, 0, 0)),
        ],
        out_shape=[jax.ShapeDtypeStruct((b, t, NSA_Q), F32), jax.ShapeDtypeStruct((b, g, nq, LANES, TQ), BF16)],
        compiler_params=_cparams(("parallel", "parallel", "parallel")),
        name="cmp_attention",
    )(proj, kc, vc_t, bias_c, overlap_t)


def _nsa_attn_body(q_ref, k_ref, vt_ref, bias_ref, *rest, selected, n_bias):
    if selected:
        sel_ref, o_ref, s_ref, selv_ref = rest
        selv_ref[...] = sel_ref[0, 0, 0].astype(F32)
    else:
        o_ref, s_ref = rest
    i = pl.program_id(2)
    q = _heads_on_lanes(q_ref[0])
    d, w = q.shape
    nq = vt_ref.shape[2]
    key = lax.broadcasted_iota(jnp.int32, (TQ, w), 0)
    qk = jnp.bitwise_and(lax.broadcasted_iota(jnp.int32, (TQ, w), 1), TQ - 1) - key

    def scores(kt, slot, live, low=None, high=None):
        kt_ld = jnp.clip(kt, 0, nq - 1)
        ks = k_ref[0, 0, pl.ds(pl.multiple_of(kt_ld * TQ, TQ), TQ), :]
        s = _dot(ks, q) + bias_ref[0, jnp.clip(i - kt, 0, n_bias - 1)]
        if selected:
            per = TQ // SEL_LEN
            rows = [selv_ref[pl.ds(kt_ld * per + c, 1), :] for c in range(per)]
            if live is not None:
                rows, live = [jnp.where(live, r, 0.0) for r in rows], None
            hit = rows[per - 1]
            for c in range(per - 2, -1, -1):
                hit = jnp.where(key[:, :TQ] < (c + 1) * SEL_LEN, rows[c], hit)
            ok = hit > 0.5
            s = jnp.concatenate([jnp.where(ok, s[:, h * TQ:(h + 1) * TQ], NEG) for h in range(NSA_HPG)], axis=1)
        if low is not None:
            s = jnp.where(qk >= low, s, NEG)
        if high is not None:
            s = jnp.where(qk < high, s, NEG)
        if live is not None:
            s = jnp.where(live, s, NEG)
        s_ref[slot] = s
        return _fold_rows(s, jnp.max)

    def weights(kt, slot, m):
        p = jnp.exp(s_ref[slot] - m)
        return _fold_rows(p, jnp.sum), _dot(vt_ref[0, 0, jnp.clip(kt, 0, nq - 1)], p.astype(BF16))

    mx = jnp.full((SUBLANES, w), NEG, F32)
    l8 = jnp.zeros((SUBLANES, w), F32)
    acc = jnp.zeros((d, w), F32)
    if selected:
        n_it = (i + NSA_TILES_PER_ITER - 1) // NSA_TILES_PER_ITER
        diag = s_ref.shape[0] - 1

        def pass1(it, mx):
            for u in range(NSA_TILES_PER_ITER):
                kt = it * NSA_TILES_PER_ITER + u
                mx = jnp.maximum(mx, scores(kt, kt, kt < i))
            return mx

        mx = lax.fori_loop(0, n_it, pass1, mx)
        m = jnp.max(jnp.maximum(mx, scores(i, diag, None, low=0)), axis=0, keepdims=True)

        def pass2(it, carry):
            l8, acc = carry
            for u in range(NSA_TILES_PER_ITER):
                kt = it * NSA_TILES_PER_ITER + u
                dl, da = weights(kt, kt, m)
                l8, acc = l8 + dl, acc + da
            return l8, acc

        l8, acc = lax.fori_loop(0, n_it, pass2, (l8, acc))
        dl, da = weights(i, diag, m)
        l8, acc = l8 + dl, acc + da
    else:
        n_win = s_ref.shape[0]
        tiles = [(i - (n_win - 1) + u, u) for u in range(n_win)]
        for kt, u in tiles:
            mx = jnp.maximum(mx, scores(kt, u, kt >= 0, low=0 if u == n_win - 1 else None,
                                        high=0 if u == 0 else None))
        m = jnp.max(mx, axis=0, keepdims=True)
        for kt, u in tiles:
            dl, da = weights(kt, u, m)
            l8, acc = l8 + dl, acc + da
    o_ref[0] = _heads_to_tokens(acc / jnp.sum(l8, axis=0, keepdims=True))


def _nsa_attention(proj, col_q, k, v_t, bias_t, sel=None):
    b, g, t, d = k.shape
    nq = t // TQ
    w = NSA_HPG * TQ
    n_bias = bias_t.shape[1]
    selected = sel is not None
    in_specs = [
        _q_spec(col_q),
        pl.BlockSpec((1, 1, t, d), lambda bi, gi, i: (bi, gi, 0, 0)),
        pl.BlockSpec((1, 1, nq, d, TQ), lambda bi, gi, i: (bi, gi, 0, 0, 0)),
        pl.BlockSpec((1, n_bias, TQ, w), lambda bi, gi, i: (gi, 0, 0, 0)),
    ]
    args = [proj, k, v_t, bias_t]
    scratch = [pltpu.VMEM((nq + NSA_TILES_PER_ITER if selected else WINDOW // TQ + 1, TQ, w), F32)]
    if selected:
        in_specs.append(pl.BlockSpec((1, 1, 1, LANES, TQ), lambda bi, gi, i: (bi, gi, i, 0, 0)))
        args.append(sel)
        scratch.append(pltpu.VMEM((LANES, TQ), F32))
    return pl.pallas_call(
        functools.partial(_nsa_attn_body, selected=selected, n_bias=n_bias),
        grid=(b, g, nq),
        in_specs=in_specs,
        out_specs=_o_spec(),
        out_shape=jax.ShapeDtypeStruct((b, t, NSA_Q), F32),
        scratch_shapes=scratch,
        compiler_params=_cparams(("parallel", "parallel", "parallel")),
        name="sel_attention" if selected else "win_attention",
    )(*args)


def _sb_body(q_ref, k_ref, v_ref, uo_ref, o_ref, c_ref):
    i = pl.program_id(2)
    uo = uo_ref[...]
    hb = c_ref.shape[0]
    pairs = range(hb // 2)
    row = lax.broadcasted_iota(jnp.int32, (TQ, TQ), 0)
    col = lax.broadcasted_iota(jnp.int32, (TQ, TQ), 1)
    rc = row - col
    first = col < HEAD_DIM
    c_ref[...] = jnp.zeros_like(c_ref)
    o_ref[...] = jnp.zeros_like(o_ref)
    qs = []
    for p in pairs:
        qp = q_ref[0, :, p * LANES:(p + 1) * LANES]
        zero = jnp.zeros_like(qp)
        qs += [jnp.where(first, qp, zero), jnp.where(first, zero, qp)]

    def cond(st):
        kt, cmax = st
        return (kt >= 0) & (cmax > SB_SKIP_LOG)

    def body(st):
        kt, _ = st
        start = pl.multiple_of(kt * TQ, TQ)
        mask = ((i - kt) * TQ + rc) > 0
        heads = range(hb)
        ks = [k_ref[0, pl.ds(start, TQ), p * LANES:(p + 1) * LANES] for p in pairs]
        vs = [v_ref[0, pl.ds(start, TQ), p * LANES:(p + 1) * LANES] for p in pairs]
        zs = [_dot_nt(qs[h], ks[h // 2]) for h in heads]
        lbs = [jnp.minimum(z, 0.0) - jnp.log(1.0 + jnp.exp(-jnp.abs(z))) for z in zs]
        parts = [_split_bf16(jnp.where(mask, lbs[h] - zs[h], 0.0)) for h in heads]
        sfxs = [_dot(hi, uo) + _dot(lo, uo) for hi, lo in parts]
        ws = [jnp.where(mask, jnp.exp(lbs[h] + sfxs[h][:, :TQ] + c_ref[h]), 0.0).astype(BF16) for h in heads]
        cm = None
        for p in pairs:
            o_ref[0, :, p * LANES:(p + 1) * LANES] += jnp.where(first, _dot(ws[2 * p], vs[p]),
                                                                _dot(ws[2 * p + 1], vs[p]))
        for h in heads:
            c = c_ref[h] + sfxs[h][:, TQ:]
            c_ref[h] = c
            cm = c if cm is None else jnp.maximum(cm, c)
        return kt - 1, jnp.max(cm)

    lax.while_loop(cond, body, (i, jnp.float32(0.0)))


def _stick_breaking(proj, col_q, col_k, col_v):
    b, t, _ = proj.shape
    hb = SB_HEADS_PER_STEP
    wb = hb * HEAD_DIM
    tri = np.triu(np.ones((TQ, TQ), np.float32), 0).T - np.eye(TQ, dtype=np.float32)
    uo = jnp.asarray(np.concatenate([tri, np.ones((TQ, TQ), np.float32)], axis=1), BF16)
    return pl.pallas_call(
        _sb_body,
        grid=(b, SB_HEADS // hb, t // TQ),
        in_specs=[
            pl.BlockSpec((1, TQ, wb), lambda bi, hi, i: (bi, i, col_q // wb + hi)),
            pl.BlockSpec((1, t, wb), lambda bi, hi, i: (bi, 0, col_k // wb + hi)),
            pl.BlockSpec((1, t, wb), lambda bi, hi, i: (bi, 0, col_v // wb + hi)),
            pl.BlockSpec((TQ, 2 * TQ), lambda bi, hi, i: (0, 0)),
        ],
        out_specs=pl.BlockSpec((1, TQ, wb), lambda bi, hi, i: (bi, i, hi)),
        out_shape=jax.ShapeDtypeStruct((b, t, SB_W), F32),
        scratch_shapes=[pltpu.VMEM((hb, TQ, TQ), F32)],
        compiler_params=_cparams(("parallel", "parallel", "parallel")),
        name="stick_breaking",
    )(proj, proj, proj, uo)


def _merge_body(oc_ref, os_ref, ow_ref, osb_ref, gbr_ref, ga_ref, gb_ref, e_ref, wn_ref, wsb_ref, o_ref):
    gate = jax.nn.sigmoid(gbr_ref[...].astype(F32))
    hi, lo = _split_bf16(gate)
    e = e_ref[...]
    gexp = _dot(hi, e) + _dot(lo, e)
    o_nsa = (gexp[:, :NSA_Q] * oc_ref[...] + gexp[:, NSA_Q:2 * NSA_Q] * os_ref[...]
             + gexp[:, 2 * NSA_Q:] * ow_ref[...])
    a = _dot(o_nsa.astype(BF16), wn_ref[...])
    bm = _dot(osb_ref[...].astype(BF16), wsb_ref[...])
    merged = jax.nn.sigmoid(ga_ref[...].astype(F32)) * a + jax.nn.sigmoid(gb_ref[...].astype(F32)) * bm
    o_ref[...] = merged.astype(o_ref.dtype)


def _merge(o_c, o_s, o_w, o_sb, proj, col_gbr, col_ga, col_gb, w_nsa, w_sb, tm=256):
    n = o_c.shape[0]
    dm = w_nsa.shape[1]
    e = np.zeros((LANES, 3 * NSA_Q), np.float32)
    for j in range(3 * NSA_HEADS):
        e[j, j * HEAD_DIM:(j + 1) * HEAD_DIM] = 1.0
    row = lambda i: (i, 0)
    full = lambda i: (0, 0)
    return pl.pallas_call(
        _merge_body,
        grid=(n // tm,),
        in_specs=[
            pl.BlockSpec((tm, NSA_Q), row), pl.BlockSpec((tm, NSA_Q), row), pl.BlockSpec((tm, NSA_Q), row),
            pl.BlockSpec((tm, SB_W), row),
            pl.BlockSpec((tm, LANES), lambda i: (i, col_gbr // LANES)),
            pl.BlockSpec((tm, dm), lambda i: (i, col_ga // dm)),
            pl.BlockSpec((tm, dm), lambda i: (i, col_gb // dm)),
            pl.BlockSpec((LANES, 3 * NSA_Q), full),
            pl.BlockSpec((NSA_Q, dm), full), pl.BlockSpec((SB_W, dm), full),
        ],
        out_specs=pl.BlockSpec((tm, dm), row),
        out_shape=jax.ShapeDtypeStruct((n, dm), BF16),
        compiler_params=_cparams(("parallel",)),
        name="gated_merge",
    )(o_c, o_s, o_w, o_sb, proj, proj, proj, jnp.asarray(e, BF16), w_nsa.astype(BF16), w_sb.astype(BF16))


def _peer_scores_body(wqt_ref, xt_ref, keys_ref, s_ref):
    qt = _dot(wqt_ref[...], xt_ref[...])
    c = keys_ref.shape[-1]
    for ch in range(keys_ref.shape[0]):
        qh, ql = _split_bf16(qt[ch * c:(ch + 1) * c])
        kh, kl = _split_bf16(keys_ref[ch])
        s_ref[ch * PEER_KEYS:(ch + 1) * PEER_KEYS, :] = _dot(kh, qh) + _dot(kh, ql) + _dot(kl, qh)


def _peer_scores(wq_t, x_t, keys, tn=512):
    rq, d = wq_t.shape
    n = x_t.shape[1]
    nch, nk, c = keys.shape
    return pl.pallas_call(
        _peer_scores_body,
        grid=(n // tn,),
        in_specs=[pl.BlockSpec((rq, d), lambda i: (0, 0)), pl.BlockSpec((d, tn), lambda i: (0, i)),
                  pl.BlockSpec((nch, nk, c), lambda i: (0, 0, 0))],
        out_specs=pl.BlockSpec((nch * nk, tn), lambda i: (0, i)),
        out_shape=jax.ShapeDtypeStruct((nch * nk, n), F32),
        compiler_params=_cparams(("parallel",)),
        name="peer_scores",
    )(wq_t, x_t, keys)


def _peer_route_body(s_ref, lim0_ref, c0_ref, rank1_ref, e1_ref):
    k = PEER_TOPK
    tn = s_ref.shape[-1]
    kidx = lax.broadcasted_iota(jnp.int32, (k, tn), 0)
    lg = int(math.log2(k))
    n_a, n_b = PEER_CAND_ROWS * k, PEER_CAND_COLS * k
    row = lax.broadcasted_iota(jnp.int32, (n_a + n_b, tn), 0)
    in_rows = row < n_a
    r0 = jnp.where(in_rows, jnp.right_shift(row, lg), jnp.bitwise_and(row - n_a, k - 1))
    r1 = jnp.where(in_rows, jnp.bitwise_and(row, k - 1), jnp.right_shift(row - n_a, lg))
    reachable = ((r0 + 1) * (r1 + 1) <= k) & (in_rows | (r0 >= PEER_CAND_ROWS))
    flat = (r0 * k + r1).astype(F32)
    for h in range(PEER_HEADS):
        s0 = s_ref[(2 * h) * PEER_KEYS:(2 * h + 1) * PEER_KEYS, :]
        s1 = s_ref[(2 * h + 1) * PEER_KEYS:(2 * h + 2) * PEER_KEYS, :]
        a, rank0 = _top_rows(s0, k)
        b, rank1 = _top_rows(s1, k)
        cand = jnp.concatenate([a[r:r + 1] + b for r in range(PEER_CAND_ROWS)]
                               + [a + b[c:c + 1] for c in range(PEER_CAND_COLS)], axis=0)
        cand = jnp.where(reachable, cand, -jnp.inf)
        best, crank = _top_rows(cand, k, order=flat)
        chosen = jnp.where(crank < float(k), 1.0, 0.0)
        e = chosen * jnp.exp(jnp.minimum(cand - best[0:1], 0.0))
        z = jnp.sum(e, axis=0, keepdims=True)
        cnt = chosen[n_a:n_a + k]
        for c in range(1, PEER_CAND_COLS):
            cnt = cnt + chosen[n_a + c * k:n_a + (c + 1) * k]
        for r in range(PEER_CAND_ROWS):
            n_r = jnp.sum(chosen[r * k:(r + 1) * k], axis=0, keepdims=True)
            cnt = jnp.where(kidx == r, n_r, cnt)
        lim0 = jnp.zeros_like(s0)
        for r in range(k):
            lim0 = jnp.where(rank0 == float(r), cnt[r:r + 1], lim0)
        lim0_ref[h] = lim0
        c0_ref[h] = jnp.exp(s0 - a[0:1]) / z
        rank1_ref[h] = rank1.astype(rank1_ref.dtype)
        e1_ref[h] = jnp.exp(s1 - b[0:1]).astype(e1_ref.dtype)


def _peer_route(s_t, tn=256):
    rows, n = s_t.shape
    shp = jax.ShapeDtypeStruct((PEER_HEADS, PEER_KEYS, n), F32)
    spec = pl.BlockSpec((PEER_HEADS, PEER_KEYS, tn), lambda i: (0, 0, i))
    return pl.pallas_call(
        _peer_route_body,
        grid=(n // tn,),
        in_specs=[pl.BlockSpec((rows, tn), lambda i: (0, i))],
        out_specs=[spec, spec, spec, spec],
        out_shape=[shp, shp, shp, shp],
        compiler_params=_cparams(("parallel",)),
        name="peer_route",
    )(s_t)


def _peer_main_body(u_ref, vt_ref, xt_ref, lim0_ref, c0_ref, rank1_in, e1_in, o_ref, act_ref, w_ref, rank1_ref,
                    e1_ref, *, te, ts):
    et = pl.program_id(1)

    @pl.when(et == 0)
    def _():
        o_ref[...] = jnp.zeros_like(o_ref)
        for h in range(PEER_HEADS):
            rank1_ref[h] = rank1_in[h].astype(BF16)
            e1_ref[h] = e1_in[h].astype(BF16)

    xt = xt_ref[...]
    tn = xt.shape[1]
    n_i = ts // PEER_KEYS
    for sb in range(te // ts):
        act_ref[sb] = _dot(u_ref[sb * ts:(sb + 1) * ts, :], xt)
    pk = 2 * SUBLANES
    grp = (PEER_KEYS // pk, pk, LANES)
    for sb in range(te // ts):
        for lc in range(tn // LANES):
            cols = slice(lc * LANES, (lc + 1) * LANES)
            ss = [jnp.zeros(grp, BF16) for _ in range(n_i)]
            for h in range(PEER_HEADS):
                r1 = rank1_ref[h, :, cols].reshape(grp)
                e1 = e1_ref[h, :, cols].reshape(grp)
                for ii in range(n_i):
                    r = sb * n_i + ii
                    lim = jnp.broadcast_to(lim0_ref[h, r:r + 1, cols], (pk, LANES)).astype(BF16)
                    cc = jnp.broadcast_to(c0_ref[h, r:r + 1, cols], (pk, LANES)).astype(BF16)
                    ss[ii] = ss[ii] + jnp.where(r1 < lim[None], e1 * cc[None], jnp.zeros((), BF16))
            for ii in range(n_i):
                rows = slice(ii * PEER_KEYS, (ii + 1) * PEER_KEYS)
                g = jax.nn.gelu(act_ref[sb, rows, cols]).astype(BF16)
                w_ref[sb, rows, cols] = ss[ii].reshape(PEER_KEYS, LANES) * g
        o_ref[...] += _dot(vt_ref[0, :, sb * ts:(sb + 1) * ts], w_ref[sb])


def _peer_main(u, v_t, x_t, lim0, c0, rank1, e1, tn=512, te=1024, ts=512):
    n_exp, d = u.shape
    n = x_t.shape[1]
    tab = pl.BlockSpec((PEER_HEADS, PEER_KEYS, tn), lambda i, j: (0, 0, i))
    tab0 = pl.BlockSpec((PEER_HEADS, te // PEER_KEYS, tn), lambda i, j: (0, j, i))
    return pl.pallas_call(
        functools.partial(_peer_main_body, te=te, ts=ts),
        grid=(n // tn, n_exp // te),
        in_specs=[pl.BlockSpec((te, d), lambda i, j: (j, 0)), pl.BlockSpec((1, d, te), lambda i, j: (j, 0, 0)),
                  pl.BlockSpec((d, tn), lambda i, j: (0, i)), tab0, tab0, tab, tab],
        out_specs=pl.BlockSpec((d, tn), lambda i, j: (0, i)),
        out_shape=jax.ShapeDtypeStruct((d, n), F32),
        scratch_shapes=[pltpu.VMEM((te // ts, ts, tn), F32), pltpu.VMEM((te // ts, ts, tn), BF16),
                        pltpu.VMEM((PEER_HEADS, PEER_KEYS, tn), BF16), pltpu.VMEM((PEER_HEADS, PEER_KEYS, tn), BF16)],
        compiler_params=_cparams(("parallel", "arbitrary"), vmem_mb=56),
        name="peer_experts",
    )(u, v_t, x_t, lim0, c0, rank1, e1)


def kernel(x, attn_norm_g, w_in, cmp_k_pe, cmp_k_w1, cmp_k_w2, cmp_v_pe, cmp_v_w1, cmp_v_w2, rel_bias_table,
           w_branch_nsa, w_branch_sb, w_out, ffn_norm_g, peer_w_q, peer_sub_keys, peer_u, peer_v, final_norm_g):
    h = x
    for l in range(attn_norm_g.shape[0]):
        h = _layer(h, attn_norm_g[l], w_in[l], cmp_k_pe[l], cmp_k_w1[l], cmp_k_w2[l], cmp_v_pe[l], cmp_v_w1[l],
                   cmp_v_w2[l], rel_bias_table, w_branch_nsa[l], w_branch_sb[l], w_out[l], ffn_norm_g[l],
                   peer_w_q[l], peer_sub_keys[l], peer_u[l], peer_v[l],
                   final_norm_g if l == attn_norm_g.shape[0] - 1 else None)
    return h


def _layer(h, attn_g, w_in, ck_pe, ck_w1, ck_w2, cv_pe, cv_w1, cv_w2, rel_table, w_br_nsa, w_br_sb, w_out,
           ffn_g, pq, psk, pu, pv, final_g):
    b, t, dm = h.shape
    n = b * t
    g, hg, d = NSA_GROUPS, NSA_HPG, HEAD_DIM
    nq = t // TQ
    x2 = h.reshape(n, dm)

    o_gbr = NSA_Q + 6 * NSA_KV
    n_gbr = 3 * NSA_HEADS
    o_qb = o_gbr + n_gbr
    o_ga = o_qb + 3 * SB_W
    tn_in = 768
    packed = w_in.shape[1]
    packed_pad = -(-packed // tn_in) * tn_in
    scale = HEAD_DIM ** -0.5
    w_pack = jnp.concatenate([w_in[:, o_ga:], w_in[:, o_qb:o_qb + SB_W] * scale, w_in[:, o_qb + SB_W:o_ga],
                              w_in[:, :NSA_Q] * scale, w_in[:, NSA_Q:o_gbr], w_in[:, o_gbr:o_qb],
                              jnp.zeros((dm, packed_pad - packed), w_in.dtype)], axis=1).astype(BF16)
    a = _rmsnorm(x2, attn_g, BF16)
    proj = _matmul(a, w_pack, 1024, tn_in, BF16, name="in_proj")
    proj3 = proj.reshape(b, t, packed_pad)
    col_ga, col_gb = 0, dm
    col_qb = 2 * dm
    col_qn = col_qb + 3 * SB_W
    col_kv = col_qn + NSA_Q
    col_gbr = col_kv + 6 * NSA_KV
    assert col_qb % (SB_HEADS_PER_STEP * d) == 0 and SB_W % (SB_HEADS_PER_STEP * d) == 0
    assert col_qn % (hg * d) == 0 and col_gbr % LANES == 0 and dm % LANES == 0

    def heads_kv(z):
        return z.reshape(b, t, g, d).transpose(0, 2, 1, 3)

    def tiles_kv_t(z):
        return z.reshape(b, nq, TQ, g, d).transpose(0, 3, 1, 4, 2)

    kv_cols = [proj[:, col_kv + j * NSA_KV:col_kv + (j + 1) * NSA_KV] for j in range(6)]
    kc_tok, vc_tok = heads_kv(kv_cols[0]), heads_kv(kv_cols[1])
    ks, vs_t = heads_kv(kv_cols[2]), tiles_kv_t(kv_cols[3])
    kw, vw_t = heads_kv(kv_cols[4]), tiles_kv_t(kv_cols[5])

    n_cmp = (t - CMP_LEN) // CMP_STRIDE + 1
    n_chunk = t // CMP_STRIDE
    ncp = -(-n_cmp // LANES) * LANES
    reps = CMP_LEN // CMP_STRIDE

    def blocks(tok):
        ch = tok.reshape(b, g, n_chunk, CMP_STRIDE * d)
        ch = jnp.pad(ch, ((0, 0), (0, 0), (0, ncp + reps - 1 - n_chunk), (0, 0)))
        blk = jnp.concatenate([ch[:, :, r:r + ncp] for r in range(reps)], axis=-1)
        return blk.reshape(b * g * ncp, CMP_LEN * d)

    kc_blk = _compress(blocks(kc_tok), ck_pe.reshape(1, CMP_LEN * d), ck_w1, ck_w2, ncp).reshape(b, g, ncp, d)
    vc_blk = _compress(blocks(vc_tok), cv_pe.reshape(1, CMP_LEN * d), cv_w1, cv_w2, ncp).reshape(b, g, ncp, d)

    r_i = jnp.arange(TQ, dtype=jnp.int32)
    per_tile = TQ // CMP_STRIDE
    rows_c = ncp + per_tile * (nq - 1)
    shift_c = per_tile * (nq - 1) - jnp.arange(rows_c, dtype=jnp.int32)
    dist_c = CMP_STRIDE * shift_c[None, :, None] + r_i[None, None, :] - (CMP_LEN - 1)
    bias_c = _bias_expand(rel_table, _rel_bucket(dist_c))
    n_bias = min(nq, -(-(REL_MAX_DIST + TQ - 1) // TQ) + 1)
    dist_t = jnp.arange(n_bias, dtype=jnp.int32)[:, None, None] * TQ + r_i[None, None, :] - r_i[None, :, None]
    bias_t = _bias_expand(rel_table, _rel_bucket(dist_t))

    n_sel = t // SEL_LEN
    assert n_sel <= LANES and t % TQ == 0 and t >= WINDOW + TQ, "selection table holds one block per row of a tile"
    nsp = min(LANES, -(-n_sel // 16) * 16)
    c_start = np.arange(ncp) * CMP_STRIDE
    s_start = np.arange(nsp) * SEL_LEN
    overlap_t = np.maximum(np.minimum(c_start[None, :] + CMP_LEN, s_start[:, None] + SEL_LEN)
                           - np.maximum(c_start[None, :], s_start[:, None]), 0).astype(np.float32) / CMP_LEN
    overlap_t[:, n_cmp:] = 0.0
    overlap_t[n_sel:, :] = 0.0
    o_c, sel = _cmp_attention(proj3, col_qn, kc_blk, vc_blk.transpose(0, 1, 3, 2), bias_c,
                              jnp.asarray(overlap_t, BF16), n_cmp, min(SEL_TOPK, n_sel))
    o_s = _nsa_attention(proj3, col_qn, ks, vs_t, bias_t, sel)
    o_w = _nsa_attention(proj3, col_qn, kw, vw_t, bias_t)
    o_sb = _stick_breaking(proj3, col_qb, col_qb + SB_W, col_qb + 2 * SB_W)

    merged = _merge(o_c.reshape(n, NSA_Q), o_s.reshape(n, NSA_Q), o_w.reshape(n, NSA_Q), o_sb.reshape(n, SB_W),
                    proj, col_gbr, col_ga, col_gb, w_br_nsa, w_br_sb)
    h1 = _matmul(merged, w_out.astype(BF16), 512, 1024, F32, res=x2, name="out_proj")

    xn_t = _rmsnorm_t(h1, ffn_g, BF16)
    keys = psk.reshape(PEER_HEADS * 2, PEER_KEYS, -1)
    s_t = _peer_scores(pq.T.astype(BF16), xn_t, keys)
    lim0, c0, rank1, e1 = _peer_route(s_t)
    te = 1024
    pv_t = pv.reshape(-1, te, dm).transpose(0, 2, 1).astype(BF16)
    ffn_t = _peer_main(pu.astype(BF16), pv_t, xn_t, lim0, c0, rank1, e1, te=te)
    if final_g is None:
        return (h1 + ffn_t.T).reshape(b, t, dm)
    return _add_rmsnorm(h1, ffn_t, final_g).reshape(b, t, dm)
```

```python
import functools
import math

import jax
import jax.numpy as jnp
import numpy as np
from jax import lax
from jax.experimental import pallas as pl
from jax.experimental.pallas import tpu as pltpu

F32 = jnp.float32
BF16 = jnp.bfloat16

HEAD_DIM = 64
NSA_HEADS = 16
NSA_GROUPS = 4
NSA_HPG = NSA_HEADS // NSA_GROUPS
SB_HEADS = 16
CMP_LEN = 32
CMP_STRIDE = 16
SEL_LEN = 64
SEL_TOPK = 16
N_LOCAL_SEL = 2
SEL_FORCE_BONUS = 1e4
WINDOW = 512
REL_BUCKETS = 32
REL_MAX_DIST = 1024
PEER_HEADS = 8
PEER_KEYS = 128
PEER_TOPK = 16
EPS = 1e-6
NEG = -1e30

NSA_Q = NSA_HEADS * HEAD_DIM
NSA_KV = NSA_GROUPS * HEAD_DIM
SB_W = SB_HEADS * HEAD_DIM

LANES = 128
SUBLANES = 8
TQ = 128
NSA_TILES_PER_ITER = 4
SB_HEADS_PER_STEP = 16
SB_SKIP_LOG = -110.0
PEER_CAND_ROWS = 4
PEER_CAND_COLS = 3
assert all(r0 < PEER_CAND_ROWS or r1 < PEER_CAND_COLS for r0 in range(PEER_TOPK) for r1 in range(PEER_TOPK)
           if (r0 + 1) * (r1 + 1) <= PEER_TOPK)

_NT = (((1,), (1,)), ((), ()))


def _cparams(sem, vmem_mb=48):
    return pltpu.CompilerParams(dimension_semantics=sem, vmem_limit_bytes=vmem_mb * 1024 * 1024)


def _dot(a, b):
    return jnp.dot(a, b, preferred_element_type=F32)


def _dot_nt(a, b):
    return lax.dot_general(a, b, _NT, preferred_element_type=F32)


def _split_bf16(x):
    hi = x.astype(BF16)
    lo = (x - hi.astype(F32)).astype(BF16)
    return hi, lo


def _fold_rows(x, op):
    r, l = x.shape
    return op(x.reshape(r // SUBLANES, SUBLANES, l), axis=0)


def _rmsnorm_body(x_ref, g_ref, o_ref):
    x = x_ref[...]
    y = x * lax.rsqrt(jnp.mean(x * x, axis=-1, keepdims=True) + EPS)
    o_ref[...] = (y * g_ref[...]).astype(o_ref.dtype)


def _rmsnorm(x, g, out_dtype, tm=512):
    n, d = x.shape
    return pl.pallas_call(
        _rmsnorm_body,
        grid=(n // tm,),
        in_specs=[pl.BlockSpec((tm, d), lambda i: (i, 0)), pl.BlockSpec((1, d), lambda i: (0, 0))],
        out_specs=pl.BlockSpec((tm, d), lambda i: (i, 0)),
        out_shape=jax.ShapeDtypeStruct((n, d), out_dtype),
        compiler_params=_cparams(("parallel",)),
        name="rmsnorm",
    )(x, g.reshape(1, d))


def _rmsnorm_t_body(x_ref, g_ref, o_ref):
    x = x_ref[...]
    y = x * lax.rsqrt(jnp.mean(x * x, axis=-1, keepdims=True) + EPS)
    o_ref[...] = (y * g_ref[...]).T.astype(o_ref.dtype)


def _rmsnorm_t(x, g, out_dtype, tm=512):
    n, d = x.shape
    return pl.pallas_call(
        _rmsnorm_t_body,
        grid=(n // tm,),
        in_specs=[pl.BlockSpec((tm, d), lambda i: (i, 0)), pl.BlockSpec((1, d), lambda i: (0, 0))],
        out_specs=pl.BlockSpec((d, tm), lambda i: (0, i)),
        out_shape=jax.ShapeDtypeStruct((d, n), out_dtype),
        compiler_params=_cparams(("parallel",)),
        name="rmsnorm_t",
    )(x, g.reshape(1, d))


def _add_rmsnorm_body(x_ref, yt_ref, g_ref, o_ref):
    x = x_ref[...] + yt_ref[...].T
    y = x * lax.rsqrt(jnp.mean(x * x, axis=-1, keepdims=True) + EPS)
    o_ref[...] = (y * g_ref[...]).astype(o_ref.dtype)


def _add_rmsnorm(x, y_t, g, tm=512):
    n, d = x.shape
    return pl.pallas_call(
        _add_rmsnorm_body,
        grid=(n // tm,),
        in_specs=[pl.BlockSpec((tm, d), lambda i: (i, 0)), pl.BlockSpec((d, tm), lambda i: (0, i)),
                  pl.BlockSpec((1, d), lambda i: (0, 0))],
        out_specs=pl.BlockSpec((tm, d), lambda i: (i, 0)),
        out_shape=jax.ShapeDtypeStruct((n, d), F32),
        compiler_params=_cparams(("parallel",)),
        name="add_rmsnorm",
    )(x, y_t, g.reshape(1, d))


def _mm_body(a_ref, w_ref, o_ref):
    o_ref[...] = _dot(a_ref[...], w_ref[...]).astype(o_ref.dtype)


def _mm_res_body(a_ref, w_ref, r_ref, o_ref):
    o_ref[...] = (r_ref[...] + _dot(a_ref[...], w_ref[...])).astype(o_ref.dtype)


def _matmul(a, w, tm, tn, out_dtype, res=None, name="matmul"):
    m, k = a.shape
    nc = w.shape[1]
    in_specs = [pl.BlockSpec((tm, k), lambda j, i: (i, 0)), pl.BlockSpec((k, tn), lambda j, i: (0, j))]
    args = [a, w]
    body = _mm_body
    if res is not None:
        in_specs.append(pl.BlockSpec((tm, tn), lambda j, i: (i, j)))
        args.append(res)
        body = _mm_res_body
    return pl.pallas_call(
        body,
        grid=(nc // tn, m // tm),
        in_specs=in_specs,
        out_specs=pl.BlockSpec((tm, tn), lambda j, i: (i, j)),
        out_shape=jax.ShapeDtypeStruct((m, nc), out_dtype),
        compiler_params=_cparams(("parallel", "parallel")),
        name=name,
    )(*args)


def _rel_bucket(dist):
    dist = jnp.maximum(dist, 0)
    n_exact = REL_BUCKETS // 2
    d_f = jnp.maximum(dist, 1).astype(jnp.float32)
    large = n_exact + (jnp.log(d_f / n_exact) / math.log(REL_MAX_DIST / n_exact)
                       * (REL_BUCKETS - n_exact)).astype(jnp.int32)
    large = jnp.minimum(large, REL_BUCKETS - 1)
    return jnp.bitwise_and(jnp.where(dist < n_exact, dist, large), REL_BUCKETS - 1)


def _bias_expand_body(tab_ref, bkt_ref, o_ref):
    h = pl.program_id(1)
    bkt = bkt_ref[0]
    acc = jnp.full(bkt.shape, tab_ref[0, h], F32)
    for k in range(1, REL_BUCKETS):
        acc = jnp.where(bkt >= k, tab_ref[k, h], acc)
    o_ref[0, 0] = acc


def _bias_expand(table, bucket):
    r, rows, _ = bucket.shape
    return pl.pallas_call(
        _bias_expand_body,
        grid=(r, NSA_HEADS),
        in_specs=[pl.BlockSpec(memory_space=pltpu.SMEM), pl.BlockSpec((1, rows, TQ), lambda i, h: (i, 0, 0))],
        out_specs=pl.BlockSpec((1, 1, rows, TQ), lambda i, h: (h // NSA_HPG, i, 0, h % NSA_HPG)),
        out_shape=jax.ShapeDtypeStruct((NSA_GROUPS, r, rows, NSA_HPG * TQ), F32),
        compiler_params=_cparams(("parallel", "parallel")),
        name="bias_expand",
    )(table, bucket)


def _top_rows(x, k, order=None):
    r, l = x.shape
    ridx = lax.broadcasted_iota(jnp.int32, (r, l), 0).astype(F32) if order is None else order
    kidx = lax.broadcasted_iota(jnp.int32, (k, l), 0)
    rank = jnp.full((r, l), float(k), F32)
    vals = jnp.zeros((k, l), F32)
    cur = x
    for j in range(k):
        m = jnp.max(cur, axis=0, keepdims=True)
        idx = jnp.min(jnp.where(cur == m, ridx, jnp.inf), axis=0, keepdims=True)
        pick = ridx == idx
        rank = jnp.where(pick, float(j), rank)
        cur = jnp.where(pick, -jnp.inf, cur)
        vals = jnp.where(kidx == j, m, vals)
    return vals, rank


def _compress_body(blk_ref, pe_ref, w1_ref, w2_ref, o_ref):
    blk = (blk_ref[...].astype(F32) + pe_ref[...]).astype(BF16)
    h = jax.nn.gelu(_dot(blk, w1_ref[...]))
    o_ref[...] = _dot(h.astype(BF16), w2_ref[...]).astype(o_ref.dtype)


def _compress(blk, pe_flat, w1, w2, tr):
    r, kd = blk.shape
    hid = w1.shape[1]
    d = w2.shape[1]
    return pl.pallas_call(
        _compress_body,
        grid=(r // tr,),
        in_specs=[pl.BlockSpec((tr, kd), lambda i: (i, 0)), pl.BlockSpec((1, kd), lambda i: (0, 0)),
                  pl.BlockSpec((kd, hid), lambda i: (0, 0)), pl.BlockSpec((hid, d), lambda i: (0, 0))],
        out_specs=pl.BlockSpec((tr, d), lambda i: (i, 0)),
        out_shape=jax.ShapeDtypeStruct((r, d), BF16),
        compiler_params=_cparams(("parallel",)),
        name="compress_mlp",
    )(blk, pe_flat, w1.astype(BF16), w2.astype(BF16))


def _heads_on_lanes(q_blk):
    qt = q_blk.astype(F32).T
    d = qt.shape[0] // NSA_HPG
    return jnp.concatenate([qt[h * d:(h + 1) * d] for h in range(NSA_HPG)], axis=1).astype(BF16)


def _heads_to_tokens(o):
    return jnp.concatenate([o[:, h * TQ:(h + 1) * TQ] for h in range(NSA_HPG)], axis=0).T


def _cmp_attn_body(q_ref, kc_ref, vct_ref, bias_ref, ovt_ref, o_ref, sel_ref, qt_ref, *, n_cmp, n_top):
    i = pl.program_id(2)
    nq = pl.num_programs(2)
    q = _heads_on_lanes(q_ref[0])
    qt_ref[0, 0, 0] = q
    kc = kc_ref[0, 0]
    ncp, w = kc.shape[0], q.shape[1]
    blk = lax.broadcasted_iota(jnp.int32, (ncp, w), 0)
    t = i * TQ + jnp.bitwise_and(lax.broadcasted_iota(jnp.int32, (ncp, w), 1), TQ - 1)
    valid = ((t - CMP_STRIDE * blk - (CMP_LEN - 1)) >= 0) & (blk < n_cmp)
    off = pl.multiple_of((nq - 1 - i) * (TQ // CMP_STRIDE), TQ // CMP_STRIDE)
    s = jnp.where(valid, _dot(kc, q) + bias_ref[0, 0, pl.ds(off, ncp), :], NEG)
    m = jnp.max(s, axis=0, keepdims=True)
    e = jnp.where(valid, jnp.exp(s - m), 0.0)
    l = jnp.sum(e, axis=0, keepdims=True)
    p = e / jnp.maximum(l, 1e-30)
    o_ref[0] = _heads_to_tokens(_dot(vct_ref[0, 0], p.astype(BF16)))
    psum = p[:, :TQ]
    for h in range(1, NSA_HPG):
        psum = psum + p[:, h * TQ:(h + 1) * TQ]
    hi, lo = _split_bf16(psum)
    ovt = ovt_ref[...]
    imp = _dot(ovt, hi) + _dot(ovt, lo)
    nsp = imp.shape[0]
    j = lax.broadcasted_iota(jnp.int32, (nsp, TQ), 0)
    cur = jnp.right_shift(i * TQ + lax.broadcasted_iota(jnp.int32, (nsp, TQ), 1), int(math.log2(SEL_LEN)))
    gap = cur - j
    forced = (j == 0) | ((gap >= 0) & (gap < N_LOCAL_SEL))
    allowed = j <= cur
    score = jnp.where(allowed, imp + SEL_FORCE_BONUS * forced.astype(F32), -jnp.inf)
    _, rank = _top_rows(score, n_top)
    sel = jnp.where((rank < float(n_top)) & allowed, 1.0, 0.0)
    if nsp < LANES:
        sel = jnp.concatenate([sel, jnp.zeros((LANES - nsp, TQ), F32)], axis=0)
    sel_ref[0, 0, 0] = sel.astype(sel_ref.dtype)


def _q_spec(col_q):
    wq = NSA_HPG * HEAD_DIM
    return pl.BlockSpec((1, TQ, wq), lambda bi, gi, i: (bi, i, col_q // wq + gi))


def _o_spec():
    return pl.BlockSpec((1, TQ, NSA_HPG * HEAD_DIM), lambda bi, gi, i: (bi, i, gi))


def _cmp_attention(proj, col_q, kc, vc_t, bias_c, overlap_t, n_cmp, n_top):
    b, t, _ = proj.shape
    g, ncp, d = kc.shape[1:]
    nq = t // TQ
    w = NSA_HPG * TQ
    return pl.pallas_call(
        functools.partial(_cmp_attn_body, n_cmp=n_cmp, n_top=n_top),
        grid=(b, g, nq),
        in_specs=[
            _q_spec(col_q),
            pl.BlockSpec((1, 1, ncp, d), lambda bi, gi, i: (bi, gi, 0, 0)),
            pl.BlockSpec((1, 1, d, ncp), lambda bi, gi, i: (bi, gi, 0, 0)),
            pl.BlockSpec((1, 1, bias_c.shape[2], w), lambda bi, gi, i: (gi, 0, 0, 0)),
            pl.BlockSpec(overlap_t.shape, lambda bi, gi, i: (0, 0)),
        ],
        out_specs=[
            _o_spec(),
            pl.BlockSpec((1, 1, 1, LANES, TQ), lambda bi, gi, i: (bi, gi, i, 0, 0)),
            pl.BlockSpec((1, 1, 1, d, w), lambda bi, gi, i: (bi, gi, i, 0, 0)),
        ],
        out_shape=[jax.ShapeDtypeStruct((b, t, NSA_Q), F32), jax.ShapeDtypeStruct((b, g, nq, LANES, TQ), BF16),
                   jax.ShapeDtypeStruct((b, g, nq, d, w), BF16)],
        compiler_params=_cparams(("parallel", "parallel", "parallel")),
        name="cmp_attention",
    )(proj, kc, vc_t, bias_c, overlap_t)


def _nsa_attn_body(q_ref, k_ref, vt_ref, bias_ref, *rest, selected, n_bias):
    if selected:
        sel_ref, o_ref, s_ref, selv_ref = rest
        selv_ref[...] = sel_ref[0, 0, 0].astype(F32)
    else:
        o_ref, s_ref = rest
    i = pl.program_id(2)
    q = q_ref[0, 0, 0]
    d, w = q.shape
    nq = vt_ref.shape[2]
    key = lax.broadcasted_iota(jnp.int32, (TQ, w), 0)
    qk = jnp.bitwise_and(lax.broadcasted_iota(jnp.int32, (TQ, w), 1), TQ - 1) - key

    def scores(kt, slot, live, low=None, high=None):
        kt_ld = jnp.clip(kt, 0, nq - 1)
        ks = k_ref[0, 0, pl.ds(pl.multiple_of(kt_ld * TQ, TQ), TQ), :]
        s = _dot(ks, q) + bias_ref[0, jnp.clip(i - kt, 0, n_bias - 1)]
        if selected:
            per = TQ // SEL_LEN
            rows = [selv_ref[pl.ds(kt_ld * per + c, 1), :] for c in range(per)]
            if live is not None:
                rows, live = [jnp.where(live, r, 0.0) for r in rows], None
            hit = rows[per - 1]
            for c in range(per - 2, -1, -1):
                hit = jnp.where(key[:, :TQ] < (c + 1) * SEL_LEN, rows[c], hit)
            ok = hit > 0.5
            s = jnp.concatenate([jnp.where(ok, s[:, h * TQ:(h + 1) * TQ], NEG) for h in range(NSA_HPG)], axis=1)
        if low is not None:
            s = jnp.where(qk >= low, s, NEG)
        if high is not None:
            s = jnp.where(qk < high, s, NEG)
        if live is not None:
            s = jnp.where(live, s, NEG)
        s_ref[slot] = s
        return _fold_rows(s, jnp.max)

    def weights(kt, slot, m):
        p = jnp.exp(s_ref[slot] - m)
        return _fold_rows(p, jnp.sum), _dot(vt_ref[0, 0, jnp.clip(kt, 0, nq - 1)], p.astype(BF16))

    mx = jnp.full((SUBLANES, w), NEG, F32)
    l8 = jnp.zeros((SUBLANES, w), F32)
    acc = jnp.zeros((d, w), F32)
    if selected:
        n_it = (i + NSA_TILES_PER_ITER - 1) // NSA_TILES_PER_ITER
        diag = s_ref.shape[0] - 1

        def pass1(it, mx):
            for u in range(NSA_TILES_PER_ITER):
                kt = it * NSA_TILES_PER_ITER + u
                mx = jnp.maximum(mx, scores(kt, kt, kt < i))
            return mx

        mx = lax.fori_loop(0, n_it, pass1, mx)
        m = jnp.max(jnp.maximum(mx, scores(i, diag, None, low=0)), axis=0, keepdims=True)

        def pass2(it, carry):
            l8, acc = carry
            for u in range(NSA_TILES_PER_ITER):
                kt = it * NSA_TILES_PER_ITER + u
                dl, da = weights(kt, kt, m)
                l8, acc = l8 + dl, acc + da
            return l8, acc

        l8, acc = lax.fori_loop(0, n_it, pass2, (l8, acc))
        dl, da = weights(i, diag, m)
        l8, acc = l8 + dl, acc + da
    else:
        n_win = s_ref.shape[0]
        tiles = [(i - (n_win - 1) + u, u) for u in range(n_win)]
        for kt, u in tiles:
            mx = jnp.maximum(mx, scores(kt, u, kt >= 0, low=0 if u == n_win - 1 else None,
                                        high=0 if u == 0 else None))
        m = jnp.max(mx, axis=0, keepdims=True)
        for kt, u in tiles:
            dl, da = weights(kt, u, m)
            l8, acc = l8 + dl, acc + da
    o_ref[0] = _heads_to_tokens(acc / jnp.sum(l8, axis=0, keepdims=True))


def _nsa_attention(q_t, k, v_t, bias_t, sel=None):
    b, g, t, d = k.shape
    nq = t // TQ
    w = NSA_HPG * TQ
    n_bias = bias_t.shape[1]
    selected = sel is not None
    in_specs = [
        pl.BlockSpec((1, 1, 1, d, w), lambda bi, gi, i: (bi, gi, i, 0, 0)),
        pl.BlockSpec((1, 1, t, d), lambda bi, gi, i: (bi, gi, 0, 0)),
        pl.BlockSpec((1, 1, nq, d, TQ), lambda bi, gi, i: (bi, gi, 0, 0, 0)),
        pl.BlockSpec((1, n_bias, TQ, w), lambda bi, gi, i: (gi, 0, 0, 0)),
    ]
    args = [q_t, k, v_t, bias_t]
    scratch = [pltpu.VMEM((nq + NSA_TILES_PER_ITER if selected else WINDOW // TQ + 1, TQ, w), F32)]
    if selected:
        in_specs.append(pl.BlockSpec((1, 1, 1, LANES, TQ), lambda bi, gi, i: (bi, gi, i, 0, 0)))
        args.append(sel)
        scratch.append(pltpu.VMEM((LANES, TQ), F32))
    return pl.pallas_call(
        functools.partial(_nsa_attn_body, selected=selected, n_bias=n_bias),
        grid=(b, g, nq),
        in_specs=in_specs,
        out_specs=_o_spec(),
        out_shape=jax.ShapeDtypeStruct((b, t, NSA_Q), F32),
        scratch_shapes=scratch,
        compiler_params=_cparams(("parallel", "parallel", "parallel")),
        name="sel_attention" if selected else "win_attention",
    )(*args)


def _sb_body(q_ref, k_ref, v_ref, uo_ref, o_ref, c_ref):
    i = pl.program_id(2)
    uo = uo_ref[...]
    hb = c_ref.shape[0]
    pairs = range(hb // 2)
    row = lax.broadcasted_iota(jnp.int32, (TQ, TQ), 0)
    col = lax.broadcasted_iota(jnp.int32, (TQ, TQ), 1)
    rc = row - col
    first = col < HEAD_DIM
    c_ref[...] = jnp.zeros_like(c_ref)
    o_ref[...] = jnp.zeros_like(o_ref)
    qs = []
    for p in pairs:
        qp = q_ref[0, :, p * LANES:(p + 1) * LANES]
        zero = jnp.zeros_like(qp)
        qs += [jnp.where(first, qp, zero), jnp.where(first, zero, qp)]

    def cond(st):
        kt, cmax = st
        return (kt >= 0) & (cmax > SB_SKIP_LOG)

    def body(st):
        kt, _ = st
        start = pl.multiple_of(kt * TQ, TQ)
        mask = ((i - kt) * TQ + rc) > 0
        heads = range(hb)
        ks = [k_ref[0, pl.ds(start, TQ), p * LANES:(p + 1) * LANES] for p in pairs]
        vs = [v_ref[0, pl.ds(start, TQ), p * LANES:(p + 1) * LANES] for p in pairs]
        zs = [_dot_nt(qs[h], ks[h // 2]) for h in heads]
        lbs = [jnp.minimum(z, 0.0) - jnp.log(1.0 + jnp.exp(-jnp.abs(z))) for z in zs]
        parts = [_split_bf16(jnp.where(mask, lbs[h] - zs[h], 0.0)) for h in heads]
        sfxs = [_dot(hi, uo) + _dot(lo, uo) for hi, lo in parts]
        ws = [jnp.where(mask, jnp.exp(lbs[h] + sfxs[h][:, :TQ] + c_ref[h]), 0.0).astype(BF16) for h in heads]
        cm = None
        for p in pairs:
            o_ref[0, :, p * LANES:(p + 1) * LANES] += jnp.where(first, _dot(ws[2 * p], vs[p]),
                                                                _dot(ws[2 * p + 1], vs[p]))
        for h in heads:
            c = c_ref[h] + sfxs[h][:, TQ:]
            c_ref[h] = c
            cm = c if cm is None else jnp.maximum(cm, c)
        return kt - 1, jnp.max(cm)

    lax.while_loop(cond, body, (i, jnp.float32(0.0)))


def _stick_breaking(proj, col_q, col_k, col_v):
    b, t, _ = proj.shape
    hb = SB_HEADS_PER_STEP
    wb = hb * HEAD_DIM
    tri = np.triu(np.ones((TQ, TQ), np.float32), 0).T - np.eye(TQ, dtype=np.float32)
    uo = jnp.asarray(np.concatenate([tri, np.ones((TQ, TQ), np.float32)], axis=1), BF16)
    return pl.pallas_call(
        _sb_body,
        grid=(b, SB_HEADS // hb, t // TQ),
        in_specs=[
            pl.BlockSpec((1, TQ, wb), lambda bi, hi, i: (bi, i, col_q // wb + hi)),
            pl.BlockSpec((1, t, wb), lambda bi, hi, i: (bi, 0, col_k // wb + hi)),
            pl.BlockSpec((1, t, wb), lambda bi, hi, i: (bi, 0, col_v // wb + hi)),
            pl.BlockSpec((TQ, 2 * TQ), lambda bi, hi, i: (0, 0)),
        ],
        out_specs=pl.BlockSpec((1, TQ, wb), lambda bi, hi, i: (bi, i, hi)),
        out_shape=jax.ShapeDtypeStruct((b, t, SB_W), F32),
        scratch_shapes=[pltpu.VMEM((hb, TQ, TQ), F32)],
        compiler_params=_cparams(("parallel", "parallel", "parallel")),
        name="stick_breaking",
    )(proj, proj, proj, uo)


def _merge_body(oc_ref, os_ref, ow_ref, osb_ref, gbr_ref, ga_ref, gb_ref, e_ref, wn_ref, wsb_ref, o_ref):
    gate = jax.nn.sigmoid(gbr_ref[...].astype(F32))
    hi, lo = _split_bf16(gate)
    e = e_ref[...]
    gexp = _dot(hi, e) + _dot(lo, e)
    o_nsa = (gexp[:, :NSA_Q] * oc_ref[...] + gexp[:, NSA_Q:2 * NSA_Q] * os_ref[...]
             + gexp[:, 2 * NSA_Q:] * ow_ref[...])
    a = _dot(o_nsa.astype(BF16), wn_ref[...])
    bm = _dot(osb_ref[...].astype(BF16), wsb_ref[...])
    merged = jax.nn.sigmoid(ga_ref[...].astype(F32)) * a + jax.nn.sigmoid(gb_ref[...].astype(F32)) * bm
    o_ref[...] = merged.astype(o_ref.dtype)


def _merge(o_c, o_s, o_w, o_sb, proj, col_gbr, col_ga, col_gb, w_nsa, w_sb, tm=256):
    n = o_c.shape[0]
    dm = w_nsa.shape[1]
    e = np.zeros((LANES, 3 * NSA_Q), np.float32)
    for j in range(3 * NSA_HEADS):
        e[j, j * HEAD_DIM:(j + 1) * HEAD_DIM] = 1.0
    row = lambda i: (i, 0)
    full = lambda i: (0, 0)
    return pl.pallas_call(
        _merge_body,
        grid=(n // tm,),
        in_specs=[
            pl.BlockSpec((tm, NSA_Q), row), pl.BlockSpec((tm, NSA_Q), row), pl.BlockSpec((tm, NSA_Q), row),
            pl.BlockSpec((tm, SB_W), row),
            pl.BlockSpec((tm, LANES), lambda i: (i, col_gbr // LANES)),
            pl.BlockSpec((tm, dm), lambda i: (i, col_ga // dm)),
            pl.BlockSpec((tm, dm), lambda i: (i, col_gb // dm)),
            pl.BlockSpec((LANES, 3 * NSA_Q), full),
            pl.BlockSpec((NSA_Q, dm), full), pl.BlockSpec((SB_W, dm), full),
        ],
        out_specs=pl.BlockSpec((tm, dm), row),
        out_shape=jax.ShapeDtypeStruct((n, dm), BF16),
        compiler_params=_cparams(("parallel",)),
        name="gated_merge",
    )(o_c, o_s, o_w, o_sb, proj, proj, proj, jnp.asarray(e, BF16), w_nsa.astype(BF16), w_sb.astype(BF16))


def _peer_scores_body(wqt_ref, xt_ref, keys_ref, s_ref):
    qt = _dot(wqt_ref[...], xt_ref[...])
    c = keys_ref.shape[-1]
    for ch in range(keys_ref.shape[0]):
        qh, ql = _split_bf16(qt[ch * c:(ch + 1) * c])
        kh, kl = _split_bf16(keys_ref[ch])
        s_ref[ch * PEER_KEYS:(ch + 1) * PEER_KEYS, :] = _dot(kh, qh) + _dot(kh, ql) + _dot(kl, qh)


def _peer_scores(wq_t, x_t, keys, tn=512):
    rq, d = wq_t.shape
    n = x_t.shape[1]
    nch, nk, c = keys.shape
    return pl.pallas_call(
        _peer_scores_body,
        grid=(n // tn,),
        in_specs=[pl.BlockSpec((rq, d), lambda i: (0, 0)), pl.BlockSpec((d, tn), lambda i: (0, i)),
                  pl.BlockSpec((nch, nk, c), lambda i: (0, 0, 0))],
        out_specs=pl.BlockSpec((nch * nk, tn), lambda i: (0, i)),
        out_shape=jax.ShapeDtypeStruct((nch * nk, n), F32),
        compiler_params=_cparams(("parallel",)),
        name="peer_scores",
    )(wq_t, x_t, keys)


def _peer_route_body(s_ref, lim0_ref, c0_ref, rank1_ref, e1_ref):
    k = PEER_TOPK
    tn = s_ref.shape[-1]
    kidx = lax.broadcasted_iota(jnp.int32, (k, tn), 0)
    lg = int(math.log2(k))
    n_a, n_b = PEER_CAND_ROWS * k, PEER_CAND_COLS * k
    row = lax.broadcasted_iota(jnp.int32, (n_a + n_b, tn), 0)
    in_rows = row < n_a
    r0 = jnp.where(in_rows, jnp.right_shift(row, lg), jnp.bitwise_and(row - n_a, k - 1))
    r1 = jnp.where(in_rows, jnp.bitwise_and(row, k - 1), jnp.right_shift(row - n_a, lg))
    reachable = ((r0 + 1) * (r1 + 1) <= k) & (in_rows | (r0 >= PEER_CAND_ROWS))
    flat = (r0 * k + r1).astype(F32)
    for h in range(PEER_HEADS):
        s0 = s_ref[(2 * h) * PEER_KEYS:(2 * h + 1) * PEER_KEYS, :]
        s1 = s_ref[(2 * h + 1) * PEER_KEYS:(2 * h + 2) * PEER_KEYS, :]
        a, rank0 = _top_rows(s0, k)
        b, rank1 = _top_rows(s1, k)
        cand = jnp.concatenate([a[r:r + 1] + b for r in range(PEER_CAND_ROWS)]
                               + [a + b[c:c + 1] for c in range(PEER_CAND_COLS)], axis=0)
        cand = jnp.where(reachable, cand, -jnp.inf)
        best, crank = _top_rows(cand, k, order=flat)
        chosen = jnp.where(crank < float(k), 1.0, 0.0)
        e = chosen * jnp.exp(jnp.minimum(cand - best[0:1], 0.0))
        z = jnp.sum(e, axis=0, keepdims=True)
        cnt = chosen[n_a:n_a + k]
        for c in range(1, PEER_CAND_COLS):
            cnt = cnt + chosen[n_a + c * k:n_a + (c + 1) * k]
        for r in range(PEER_CAND_ROWS):
            n_r = jnp.sum(chosen[r * k:(r + 1) * k], axis=0, keepdims=True)
            cnt = jnp.where(kidx == r, n_r, cnt)
        lim0 = jnp.zeros_like(s0)
        for r in range(k):
            lim0 = jnp.where(rank0 == float(r), cnt[r:r + 1], lim0)
        lim0_ref[h] = lim0
        c0_ref[h] = jnp.exp(s0 - a[0:1]) / z
        rank1_ref[h] = rank1.astype(rank1_ref.dtype)
        e1_ref[h] = jnp.exp(s1 - b[0:1]).astype(e1_ref.dtype)


def _peer_route(s_t, tn=256):
    rows, n = s_t.shape
    shp = jax.ShapeDtypeStruct((PEER_HEADS, PEER_KEYS, n), F32)
    spec = pl.BlockSpec((PEER_HEADS, PEER_KEYS, tn), lambda i: (0, 0, i))
    return pl.pallas_call(
        _peer_route_body,
        grid=(n // tn,),
        in_specs=[pl.BlockSpec((rows, tn), lambda i: (0, i))],
        out_specs=[spec, spec, spec, spec],
        out_shape=[shp, shp, shp, shp],
        compiler_params=_cparams(("parallel",)),
        name="peer_route",
    )(s_t)


def _peer_main_body(u_ref, vt_ref, xt_ref, lim0_ref, c0_ref, rank1_in, e1_in, o_ref, act_ref, w_ref, rank1_ref,
                    e1_ref, *, te, ts):
    et = pl.program_id(1)

    @pl.when(et == 0)
    def _():
        o_ref[...] = jnp.zeros_like(o_ref)
        for h in range(PEER_HEADS):
            rank1_ref[h] = rank1_in[h].astype(BF16)
            e1_ref[h] = e1_in[h].astype(BF16)

    xt = xt_ref[...]
    tn = xt.shape[1]
    n_i = ts // PEER_KEYS
    for sb in range(te // ts):
        act_ref[sb] = _dot(u_ref[sb * ts:(sb + 1) * ts, :], xt)
    pk = 2 * SUBLANES
    grp = (PEER_KEYS // pk, pk, LANES)
    for sb in range(te // ts):
        for lc in range(tn // LANES):
            cols = slice(lc * LANES, (lc + 1) * LANES)
            ss = [jnp.zeros(grp, BF16) for _ in range(n_i)]
            for h in range(PEER_HEADS):
                r1 = rank1_ref[h, :, cols].reshape(grp)
                e1 = e1_ref[h, :, cols].reshape(grp)
                for ii in range(n_i):
                    r = sb * n_i + ii
                    lim = jnp.broadcast_to(lim0_ref[h, r:r + 1, cols], (pk, LANES)).astype(BF16)
                    cc = jnp.broadcast_to(c0_ref[h, r:r + 1, cols], (pk, LANES)).astype(BF16)
                    ss[ii] = ss[ii] + jnp.where(r1 < lim[None], e1 * cc[None], jnp.zeros((), BF16))
            for ii in range(n_i):
                rows = slice(ii * PEER_KEYS, (ii + 1) * PEER_KEYS)
                g = jax.nn.gelu(act_ref[sb, rows, cols]).astype(BF16)
                w_ref[sb, rows, cols] = ss[ii].reshape(PEER_KEYS, LANES) * g
        o_ref[...] += _dot(vt_ref[0, :, sb * ts:(sb + 1) * ts], w_ref[sb])


def _peer_main(u, v_t, x_t, lim0, c0, rank1, e1, tn=512, te=1024, ts=512):
    n_exp, d = u.shape
    n = x_t.shape[1]
    tab = pl.BlockSpec((PEER_HEADS, PEER_KEYS, tn), lambda i, j: (0, 0, i))
    tab0 = pl.BlockSpec((PEER_HEADS, te // PEER_KEYS, tn), lambda i, j: (0, j, i))
    return pl.pallas_call(
        functools.partial(_peer_main_body, te=te, ts=ts),
        grid=(n // tn, n_exp // te),
        in_specs=[pl.BlockSpec((te, d), lambda i, j: (j, 0)), pl.BlockSpec((1, d, te), lambda i, j: (j, 0, 0)),
                  pl.BlockSpec((d, tn), lambda i, j: (0, i)), tab0, tab0, tab, tab],
        out_specs=pl.BlockSpec((d, tn), lambda i, j: (0, i)),
        out_shape=jax.ShapeDtypeStruct((d, n), F32),
        scratch_shapes=[pltpu.VMEM((te // ts, ts, tn), F32), pltpu.VMEM((te // ts, ts, tn), BF16),
                        pltpu.VMEM((PEER_HEADS, PEER_KEYS, tn), BF16), pltpu.VMEM((PEER_HEADS, PEER_KEYS, tn), BF16)],
        compiler_params=_cparams(("parallel", "arbitrary"), vmem_mb=56),
        name="peer_experts",
    )(u, v_t, x_t, lim0, c0, rank1, e1)


def kernel(x, attn_norm_g, w_in, cmp_k_pe, cmp_k_w1, cmp_k_w2, cmp_v_pe, cmp_v_w1, cmp_v_w2, rel_bias_table,
           w_branch_nsa, w_branch_sb, w_out, ffn_norm_g, peer_w_q, peer_sub_keys, peer_u, peer_v, final_norm_g):
    h = x
    for l in range(attn_norm_g.shape[0]):
        h = _layer(h, attn_norm_g[l], w_in[l], cmp_k_pe[l], cmp_k_w1[l], cmp_k_w2[l], cmp_v_pe[l], cmp_v_w1[l],
                   cmp_v_w2[l], rel_bias_table, w_branch_nsa[l], w_branch_sb[l], w_out[l], ffn_norm_g[l],
                   peer_w_q[l], peer_sub_keys[l], peer_u[l], peer_v[l],
                   final_norm_g if l == attn_norm_g.shape[0] - 1 else None)
    return h


def _layer(h, attn_g, w_in, ck_pe, ck_w1, ck_w2, cv_pe, cv_w1, cv_w2, rel_table, w_br_nsa, w_br_sb, w_out,
           ffn_g, pq, psk, pu, pv, final_g):
    b, t, dm = h.shape
    n = b * t
    g, hg, d = NSA_GROUPS, NSA_HPG, HEAD_DIM
    nq = t // TQ
    x2 = h.reshape(n, dm)

    o_gbr = NSA_Q + 6 * NSA_KV
    n_gbr = 3 * NSA_HEADS
    o_qb = o_gbr + n_gbr
    o_ga = o_qb + 3 * SB_W
    tn_in = 768
    packed = w_in.shape[1]
    packed_pad = -(-packed // tn_in) * tn_in
    scale = HEAD_DIM ** -0.5
    w_pack = jnp.concatenate([w_in[:, o_ga:], w_in[:, o_qb:o_qb + SB_W] * scale, w_in[:, o_qb + SB_W:o_ga],
                              w_in[:, :NSA_Q] * scale, w_in[:, NSA_Q:o_gbr], w_in[:, o_gbr:o_qb],
                              jnp.zeros((dm, packed_pad - packed), w_in.dtype)], axis=1).astype(BF16)
    a = _rmsnorm(x2, attn_g, BF16)
    proj = _matmul(a, w_pack, 2048, tn_in, BF16, name="in_proj")
    proj3 = proj.reshape(b, t, packed_pad)
    col_ga, col_gb = 0, dm
    col_qb = 2 * dm
    col_qn = col_qb + 3 * SB_W
    col_kv = col_qn + NSA_Q
    col_gbr = col_kv + 6 * NSA_KV
    assert col_qb % (SB_HEADS_PER_STEP * d) == 0 and SB_W % (SB_HEADS_PER_STEP * d) == 0
    assert col_qn % (hg * d) == 0 and col_gbr % LANES == 0 and dm % LANES == 0

    def heads_kv(z):
        return z.reshape(b, t, g, d).transpose(0, 2, 1, 3)

    def tiles_kv_t(z):
        return z.reshape(b, nq, TQ, g, d).transpose(0, 3, 1, 4, 2)

    kv_cols = [proj[:, col_kv + j * NSA_KV:col_kv + (j + 1) * NSA_KV] for j in range(6)]
    kc_tok, vc_tok = heads_kv(kv_cols[0]), heads_kv(kv_cols[1])
    ks, vs_t = heads_kv(kv_cols[2]), tiles_kv_t(kv_cols[3])
    kw, vw_t = heads_kv(kv_cols[4]), tiles_kv_t(kv_cols[5])

    n_cmp = (t - CMP_LEN) // CMP_STRIDE + 1
    n_chunk = t // CMP_STRIDE
    ncp = -(-n_cmp // LANES) * LANES
    reps = CMP_LEN // CMP_STRIDE

    def blocks(tok):
        ch = tok.reshape(b, g, n_chunk, CMP_STRIDE * d)
        ch = jnp.pad(ch, ((0, 0), (0, 0), (0, ncp + reps - 1 - n_chunk), (0, 0)))
        blk = jnp.concatenate([ch[:, :, r:r + ncp] for r in range(reps)], axis=-1)
        return blk.reshape(b * g * ncp, CMP_LEN * d)

    kc_blk = _compress(blocks(kc_tok), ck_pe.reshape(1, CMP_LEN * d), ck_w1, ck_w2, ncp).reshape(b, g, ncp, d)
    vc_blk = _compress(blocks(vc_tok), cv_pe.reshape(1, CMP_LEN * d), cv_w1, cv_w2, ncp).reshape(b, g, ncp, d)

    r_i = jnp.arange(TQ, dtype=jnp.int32)
    per_tile = TQ // CMP_STRIDE
    rows_c = ncp + per_tile * (nq - 1)
    shift_c = per_tile * (nq - 1) - jnp.arange(rows_c, dtype=jnp.int32)
    dist_c = CMP_STRIDE * shift_c[None, :, None] + r_i[None, None, :] - (CMP_LEN - 1)
    bias_c = _bias_expand(rel_table, _rel_bucket(dist_c))
    n_bias = min(nq, -(-(REL_MAX_DIST + TQ - 1) // TQ) + 1)
    dist_t = jnp.arange(n_bias, dtype=jnp.int32)[:, None, None] * TQ + r_i[None, None, :] - r_i[None, :, None]
    bias_t = _bias_expand(rel_table, _rel_bucket(dist_t))

    n_sel = t // SEL_LEN
    assert n_sel <= LANES and t % TQ == 0 and t >= WINDOW + TQ, "selection table holds one block per row of a tile"
    nsp = min(LANES, -(-n_sel // 16) * 16)
    c_start = np.arange(ncp) * CMP_STRIDE
    s_start = np.arange(nsp) * SEL_LEN
    overlap_t = np.maximum(np.minimum(c_start[None, :] + CMP_LEN, s_start[:, None] + SEL_LEN)
                           - np.maximum(c_start[None, :], s_start[:, None]), 0).astype(np.float32) / CMP_LEN
    overlap_t[:, n_cmp:] = 0.0
    overlap_t[n_sel:, :] = 0.0
    o_c, sel, q_t = _cmp_attention(proj3, col_qn, kc_blk, vc_blk.transpose(0, 1, 3, 2), bias_c,
                                   jnp.asarray(overlap_t, BF16), n_cmp, min(SEL_TOPK, n_sel))
    o_s = _nsa_attention(q_t, ks, vs_t, bias_t, sel)
    o_w = _nsa_attention(q_t, kw, vw_t, bias_t)
    o_sb = _stick_breaking(proj3, col_qb, col_qb + SB_W, col_qb + 2 * SB_W)

    merged = _merge(o_c.reshape(n, NSA_Q), o_s.reshape(n, NSA_Q), o_w.reshape(n, NSA_Q), o_sb.reshape(n, SB_W),
                    proj, col_gbr, col_ga, col_gb, w_br_nsa, w_br_sb)
    h1 = _matmul(merged, w_out.astype(BF16), 1024, 1024, F32, res=x2, name="out_proj")

    xn_t = _rmsnorm_t(h1, ffn_g, BF16)
    keys = psk.reshape(PEER_HEADS * 2, PEER_KEYS, -1)
    s_t = _peer_scores(pq.T.astype(BF16), xn_t, keys)
    lim0, c0, rank1, e1 = _peer_route(s_t)
    te = 1024
    pv_t = pv.reshape(-1, te, dm).transpose(0, 2, 1).astype(BF16)
    ffn_t = _peer_main(pu.astype(BF16), pv_t, xn_t, lim0, c0, rank1, e1, te=te)
    if final_g is None:
        return (h1 + ffn_t.T).reshape(b, t, dm)
    return _add_rmsnorm(h1, ffn_t, final_g).reshape(b, t, dm)
```

```python
import functools
import math

import jax
import jax.numpy as jnp
import numpy as np
from jax import lax
from jax.experimental import pallas as pl
from jax.experimental.pallas import tpu as pltpu

F32 = jnp.float32
BF16 = jnp.bfloat16

HEAD_DIM = 64
NSA_HEADS = 16
NSA_GROUPS = 4
NSA_HPG = NSA_HEADS // NSA_GROUPS
SB_HEADS = 16
CMP_LEN = 32
CMP_STRIDE = 16
SEL_LEN = 64
SEL_TOPK = 16
N_LOCAL_SEL = 2
SEL_FORCE_BONUS = 1e4
WINDOW = 512
REL_BUCKETS = 32
REL_MAX_DIST = 1024
PEER_HEADS = 8
PEER_KEYS = 128
PEER_TOPK = 16
EPS = 1e-6
NEG = -1e30

NSA_Q = NSA_HEADS * HEAD_DIM
NSA_KV = NSA_GROUPS * HEAD_DIM
SB_W = SB_HEADS * HEAD_DIM

LANES = 128
SUBLANES = 8
TQ = 128
NSA_TILES_PER_ITER = 4
SB_HEADS_PER_STEP = 16
SB_SKIP_LOG = -110.0
PEER_CAND_ROWS = 4
PEER_CAND_COLS = 3
assert all(r0 < PEER_CAND_ROWS or r1 < PEER_CAND_COLS for r0 in range(PEER_TOPK) for r1 in range(PEER_TOPK)
           if (r0 + 1) * (r1 + 1) <= PEER_TOPK)

_NT = (((1,), (1,)), ((), ()))


def _cparams(sem, vmem_mb=48):
    return pltpu.CompilerParams(dimension_semantics=sem, vmem_limit_bytes=vmem_mb * 1024 * 1024)


def _dot(a, b):
    return jnp.dot(a, b, preferred_element_type=F32)


def _dot_nt(a, b):
    return lax.dot_general(a, b, _NT, preferred_element_type=F32)


def _split_bf16(x):
    hi = x.astype(BF16)
    lo = (x - hi.astype(F32)).astype(BF16)
    return hi, lo


def _fold_rows(x, op):
    r, l = x.shape
    return op(x.reshape(r // SUBLANES, SUBLANES, l), axis=0)


def _rmsnorm_body(x_ref, g_ref, o_ref):
    x = x_ref[...]
    y = x * lax.rsqrt(jnp.mean(x * x, axis=-1, keepdims=True) + EPS)
    o_ref[...] = (y * g_ref[...]).astype(o_ref.dtype)


def _rmsnorm(x, g, out_dtype, tm=512):
    n, d = x.shape
    return pl.pallas_call(
        _rmsnorm_body,
        grid=(n // tm,),
        in_specs=[pl.BlockSpec((tm, d), lambda i: (i, 0)), pl.BlockSpec((1, d), lambda i: (0, 0))],
        out_specs=pl.BlockSpec((tm, d), lambda i: (i, 0)),
        out_shape=jax.ShapeDtypeStruct((n, d), out_dtype),
        compiler_params=_cparams(("parallel",)),
        name="rmsnorm",
    )(x, g.reshape(1, d))


def _rmsnorm_t_body(x_ref, g_ref, o_ref):
    x = x_ref[...]
    y = x * lax.rsqrt(jnp.mean(x * x, axis=-1, keepdims=True) + EPS)
    o_ref[...] = (y * g_ref[...]).T.astype(o_ref.dtype)


def _rmsnorm_t(x, g, out_dtype, tm=512):
    n, d = x.shape
    return pl.pallas_call(
        _rmsnorm_t_body,
        grid=(n // tm,),
        in_specs=[pl.BlockSpec((tm, d), lambda i: (i, 0)), pl.BlockSpec((1, d), lambda i: (0, 0))],
        out_specs=pl.BlockSpec((d, tm), lambda i: (0, i)),
        out_shape=jax.ShapeDtypeStruct((d, n), out_dtype),
        compiler_params=_cparams(("parallel",)),
        name="rmsnorm_t",
    )(x, g.reshape(1, d))


def _add_rmsnorm_body(x_ref, yt_ref, g_ref, o_ref):
    x = x_ref[...] + yt_ref[...].T
    y = x * lax.rsqrt(jnp.mean(x * x, axis=-1, keepdims=True) + EPS)
    o_ref[...] = (y * g_ref[...]).astype(o_ref.dtype)


def _add_rmsnorm(x, y_t, g, tm=512):
    n, d = x.shape
    return pl.pallas_call(
        _add_rmsnorm_body,
        grid=(n // tm,),
        in_specs=[pl.BlockSpec((tm, d), lambda i: (i, 0)), pl.BlockSpec((d, tm), lambda i: (0, i)),
                  pl.BlockSpec((1, d), lambda i: (0, 0))],
        out_specs=pl.BlockSpec((tm, d), lambda i: (i, 0)),
        out_shape=jax.ShapeDtypeStruct((n, d), F32),
        compiler_params=_cparams(("parallel",)),
        name="add_rmsnorm",
    )(x, y_t, g.reshape(1, d))


def _mm_body(a_ref, w_ref, o_ref):
    o_ref[...] = _dot(a_ref[...], w_ref[...]).astype(o_ref.dtype)


def _mm_res_body(a_ref, w_ref, r_ref, o_ref):
    o_ref[...] = (r_ref[...] + _dot(a_ref[...], w_ref[...])).astype(o_ref.dtype)


def _matmul(a, w, tm, tn, out_dtype, res=None, name="matmul"):
    m, k = a.shape
    nc = w.shape[1]
    in_specs = [pl.BlockSpec((tm, k), lambda j, i: (i, 0)), pl.BlockSpec((k, tn), lambda j, i: (0, j))]
    args = [a, w]
    body = _mm_body
    if res is not None:
        in_specs.append(pl.BlockSpec((tm, tn), lambda j, i: (i, j)))
        args.append(res)
        body = _mm_res_body
    return pl.pallas_call(
        body,
        grid=(nc // tn, m // tm),
        in_specs=in_specs,
        out_specs=pl.BlockSpec((tm, tn), lambda j, i: (i, j)),
        out_shape=jax.ShapeDtypeStruct((m, nc), out_dtype),
        compiler_params=_cparams(("parallel", "parallel")),
        name=name,
    )(*args)


def _rel_bucket(dist):
    dist = jnp.maximum(dist, 0)
    n_exact = REL_BUCKETS // 2
    d_f = jnp.maximum(dist, 1).astype(jnp.float32)
    large = n_exact + (jnp.log(d_f / n_exact) / math.log(REL_MAX_DIST / n_exact)
                       * (REL_BUCKETS - n_exact)).astype(jnp.int32)
    large = jnp.minimum(large, REL_BUCKETS - 1)
    return jnp.bitwise_and(jnp.where(dist < n_exact, dist, large), REL_BUCKETS - 1)


def _bias_expand_body(tab_ref, bkt_ref, o_ref):
    rows = bkt_ref.shape[1]
    chunk = max(c for c in range(SUBLANES, 8 * SUBLANES + 1, SUBLANES) if rows % c == 0)
    for r0 in range(0, rows, chunk):
        bkt = bkt_ref[0, r0:r0 + chunk, :]
        accs = [jnp.full(bkt.shape, tab_ref[0, h], F32) for h in range(NSA_HEADS)]
        for k in range(1, REL_BUCKETS):
            passed = bkt >= k
            accs = [jnp.where(passed, tab_ref[k, h], accs[h]) for h in range(NSA_HEADS)]
        for h in range(NSA_HEADS):
            o_ref[h // NSA_HPG, 0, r0:r0 + chunk, (h % NSA_HPG) * TQ:(h % NSA_HPG + 1) * TQ] = accs[h]


def _bias_expand(table, bucket):
    r, rows, _ = bucket.shape
    return pl.pallas_call(
        _bias_expand_body,
        grid=(r,),
        in_specs=[pl.BlockSpec(memory_space=pltpu.SMEM), pl.BlockSpec((1, rows, TQ), lambda i: (i, 0, 0))],
        out_specs=pl.BlockSpec((NSA_GROUPS, 1, rows, NSA_HPG * TQ), lambda i: (0, i, 0, 0)),
        out_shape=jax.ShapeDtypeStruct((NSA_GROUPS, r, rows, NSA_HPG * TQ), F32),
        compiler_params=_cparams(("parallel",)),
        name="bias_expand",
    )(table, bucket)


def _top_rows(x, k, order=None):
    r, l = x.shape
    ridx = lax.broadcasted_iota(jnp.int32, (r, l), 0).astype(F32) if order is None else order
    kidx = lax.broadcasted_iota(jnp.int32, (k, l), 0)
    rank = jnp.full((r, l), float(k), F32)
    vals = jnp.zeros((k, l), F32)
    cur = x
    for j in range(k):
        m = jnp.max(cur, axis=0, keepdims=True)
        idx = jnp.min(jnp.where(cur == m, ridx, jnp.inf), axis=0, keepdims=True)
        pick = ridx == idx
        rank = jnp.where(pick, float(j), rank)
        cur = jnp.where(pick, -jnp.inf, cur)
        vals = jnp.where(kidx == j, m, vals)
    return vals, rank


def _compress_body(blk_ref, pe_ref, w1_ref, w2_ref, o_ref):
    blk = (blk_ref[...].astype(F32) + pe_ref[...]).astype(BF16)
    h = jax.nn.gelu(_dot(blk, w1_ref[...]))
    o_ref[...] = _dot(h.astype(BF16), w2_ref[...]).astype(o_ref.dtype)


def _compress(blk, pe_flat, w1, w2, tr):
    r, kd = blk.shape
    hid = w1.shape[1]
    d = w2.shape[1]
    return pl.pallas_call(
        _compress_body,
        grid=(r // tr,),
        in_specs=[pl.BlockSpec((tr, kd), lambda i: (i, 0)), pl.BlockSpec((1, kd), lambda i: (0, 0)),
                  pl.BlockSpec((kd, hid), lambda i: (0, 0)), pl.BlockSpec((hid, d), lambda i: (0, 0))],
        out_specs=pl.BlockSpec((tr, d), lambda i: (i, 0)),
        out_shape=jax.ShapeDtypeStruct((r, d), BF16),
        compiler_params=_cparams(("parallel",)),
        name="compress_mlp",
    )(blk, pe_flat, w1.astype(BF16), w2.astype(BF16))


def _heads_on_lanes(q_blk):
    qt = q_blk.astype(F32).T
    d = qt.shape[0] // NSA_HPG
    return jnp.concatenate([qt[h * d:(h + 1) * d] for h in range(NSA_HPG)], axis=1).astype(BF16)


def _heads_to_tokens(o):
    return jnp.concatenate([o[:, h * TQ:(h + 1) * TQ] for h in range(NSA_HPG)], axis=0).T


def _cmp_attn_body(q_ref, kc_ref, vct_ref, bias_ref, ovt_ref, o_ref, sel_ref, qt_ref, *, n_cmp, n_top):
    i = pl.program_id(2)
    nq = pl.num_programs(2)
    q = _heads_on_lanes(q_ref[0])
    qt_ref[0, 0, 0] = q
    kc = kc_ref[0, 0]
    ncp, w = kc.shape[0], q.shape[1]
    blk = lax.broadcasted_iota(jnp.int32, (ncp, w), 0)
    t = i * TQ + jnp.bitwise_and(lax.broadcasted_iota(jnp.int32, (ncp, w), 1), TQ - 1)
    valid = ((t - CMP_STRIDE * blk - (CMP_LEN - 1)) >= 0) & (blk < n_cmp)
    off = pl.multiple_of((nq - 1 - i) * (TQ // CMP_STRIDE), TQ // CMP_STRIDE)
    s = jnp.where(valid, _dot(kc, q) + bias_ref[0, 0, pl.ds(off, ncp), :], NEG)
    m = jnp.max(s, axis=0, keepdims=True)
    e = jnp.where(valid, jnp.exp(s - m), 0.0)
    l = jnp.sum(e, axis=0, keepdims=True)
    p = e / jnp.maximum(l, 1e-30)
    o_ref[0] = _heads_to_tokens(_dot(vct_ref[0, 0], p.astype(BF16)))
    psum = p[:, :TQ]
    for h in range(1, NSA_HPG):
        psum = psum + p[:, h * TQ:(h + 1) * TQ]
    hi, lo = _split_bf16(psum)
    ovt = ovt_ref[...]
    imp = _dot(ovt, hi) + _dot(ovt, lo)
    nsp = imp.shape[0]
    j = lax.broadcasted_iota(jnp.int32, (nsp, TQ), 0)
    cur = jnp.right_shift(i * TQ + lax.broadcasted_iota(jnp.int32, (nsp, TQ), 1), int(math.log2(SEL_LEN)))
    gap = cur - j
    forced = (j == 0) | ((gap >= 0) & (gap < N_LOCAL_SEL))
    allowed = j <= cur
    score = jnp.where(allowed, imp + SEL_FORCE_BONUS * forced.astype(F32), -jnp.inf)
    _, rank = _top_rows(score, n_top)
    sel = jnp.where((rank < float(n_top)) & allowed, 1.0, 0.0)
    if nsp < LANES:
        sel = jnp.concatenate([sel, jnp.zeros((LANES - nsp, TQ), F32)], axis=0)
    sel_ref[0, 0, 0] = sel.astype(sel_ref.dtype)


def _q_spec(col_q):
    wq = NSA_HPG * HEAD_DIM
    return pl.BlockSpec((1, TQ, wq), lambda bi, gi, i: (bi, i, col_q // wq + gi))


def _o_spec():
    return pl.BlockSpec((1, TQ, NSA_HPG * HEAD_DIM), lambda bi, gi, i: (bi, i, gi))


def _cmp_attention(proj, col_q, kc, vc_t, bias_c, overlap_t, n_cmp, n_top):
    b, t, _ = proj.shape
    g, ncp, d = kc.shape[1:]
    nq = t // TQ
    w = NSA_HPG * TQ
    return pl.pallas_call(
        functools.partial(_cmp_attn_body, n_cmp=n_cmp, n_top=n_top),
        grid=(b, g, nq),
        in_specs=[
            _q_spec(col_q),
            pl.BlockSpec((1, 1, ncp, d), lambda bi, gi, i: (bi, gi, 0, 0)),
            pl.BlockSpec((1, 1, d, ncp), lambda bi, gi, i: (bi, gi, 0, 0)),
            pl.BlockSpec((1, 1, bias_c.shape[2], w), lambda bi, gi, i: (gi, 0, 0, 0)),
            pl.BlockSpec(overlap_t.shape, lambda bi, gi, i: (0, 0)),
        ],
        out_specs=[
            _o_spec(),
            pl.BlockSpec((1, 1, 1, LANES, TQ), lambda bi, gi, i: (bi, gi, i, 0, 0)),
            pl.BlockSpec((1, 1, 1, d, w), lambda bi, gi, i: (bi, gi, i, 0, 0)),
        ],
        out_shape=[jax.ShapeDtypeStruct((b, t, NSA_Q), F32), jax.ShapeDtypeStruct((b, g, nq, LANES, TQ), BF16),
                   jax.ShapeDtypeStruct((b, g, nq, d, w), BF16)],
        compiler_params=_cparams(("parallel", "parallel", "parallel")),
        name="cmp_attention",
    )(proj, kc, vc_t, bias_c, overlap_t)


def _nsa_attn_body(q_ref, k_ref, vt_ref, bias_ref, *rest, selected, n_bias):
    if selected:
        sel_ref, o_ref, s_ref, selv_ref = rest
        selv_ref[...] = sel_ref[0, 0, 0].astype(F32)
    else:
        o_ref, s_ref = rest
    i = pl.program_id(2)
    q = q_ref[0, 0, 0]
    d, w = q.shape
    nq = vt_ref.shape[2]
    key = lax.broadcasted_iota(jnp.int32, (TQ, w), 0)
    qk = jnp.bitwise_and(lax.broadcasted_iota(jnp.int32, (TQ, w), 1), TQ - 1) - key

    def scores(kt, slot, live, low=None, high=None):
        kt_ld = jnp.clip(kt, 0, nq - 1)
        ks = k_ref[0, 0, pl.ds(pl.multiple_of(kt_ld * TQ, TQ), TQ), :]
        s = _dot(ks, q) + bias_ref[0, jnp.clip(i - kt, 0, n_bias - 1)]
        if selected:
            per = TQ // SEL_LEN
            rows = [selv_ref[pl.ds(kt_ld * per + c, 1), :] for c in range(per)]
            if live is not None:
                rows, live = [jnp.where(live, r, 0.0) for r in rows], None
            hit = rows[per - 1]
            for c in range(per - 2, -1, -1):
                hit = jnp.where(key[:, :TQ] < (c + 1) * SEL_LEN, rows[c], hit)
            ok = hit > 0.5
            s = jnp.concatenate([jnp.where(ok, s[:, h * TQ:(h + 1) * TQ], NEG) for h in range(NSA_HPG)], axis=1)
        if low is not None:
            s = jnp.where(qk >= low, s, NEG)
        if high is not None:
            s = jnp.where(qk < high, s, NEG)
        if live is not None:
            s = jnp.where(live, s, NEG)
        s_ref[slot] = s
        return _fold_rows(s, jnp.max)

    def weights(kt, slot, m):
        p = jnp.exp(s_ref[slot] - m)
        return _fold_rows(p, jnp.sum), _dot(vt_ref[0, 0, jnp.clip(kt, 0, nq - 1)], p.astype(BF16))

    mx = jnp.full((SUBLANES, w), NEG, F32)
    l8 = jnp.zeros((SUBLANES, w), F32)
    acc = jnp.zeros((d, w), F32)
    if selected:
        n_it = (i + NSA_TILES_PER_ITER - 1) // NSA_TILES_PER_ITER
        diag = s_ref.shape[0] - 1

        def pass1(it, mx):
            for u in range(NSA_TILES_PER_ITER):
                kt = it * NSA_TILES_PER_ITER + u
                mx = jnp.maximum(mx, scores(kt, kt, kt < i))
            return mx

        mx = lax.fori_loop(0, n_it, pass1, mx)
        m = jnp.max(jnp.maximum(mx, scores(i, diag, None, low=0)), axis=0, keepdims=True)

        def pass2(it, carry):
            l8, acc = carry
            for u in range(NSA_TILES_PER_ITER):
                kt = it * NSA_TILES_PER_ITER + u
                dl, da = weights(kt, kt, m)
                l8, acc = l8 + dl, acc + da
            return l8, acc

        l8, acc = lax.fori_loop(0, n_it, pass2, (l8, acc))
        dl, da = weights(i, diag, m)
        l8, acc = l8 + dl, acc + da
    else:
        n_win = s_ref.shape[0]
        tiles = [(i - (n_win - 1) + u, u) for u in range(n_win)]
        for kt, u in tiles:
            mx = jnp.maximum(mx, scores(kt, u, kt >= 0, low=0 if u == n_win - 1 else None,
                                        high=0 if u == 0 else None))
        m = jnp.max(mx, axis=0, keepdims=True)
        for kt, u in tiles:
            dl, da = weights(kt, u, m)
            l8, acc = l8 + dl, acc + da
    o_ref[0] = _heads_to_tokens(acc / jnp.sum(l8, axis=0, keepdims=True))


def _nsa_attention(q_t, k, v_t, bias_t, sel=None):
    b, g, t, d = k.shape
    nq = t // TQ
    w = NSA_HPG * TQ
    n_bias = bias_t.shape[1]
    selected = sel is not None
    in_specs = [
        pl.BlockSpec((1, 1, 1, d, w), lambda bi, gi, i: (bi, gi, i, 0, 0)),
        pl.BlockSpec((1, 1, t, d), lambda bi, gi, i: (bi, gi, 0, 0)),
        pl.BlockSpec((1, 1, nq, d, TQ), lambda bi, gi, i: (bi, gi, 0, 0, 0)),
        pl.BlockSpec((1, n_bias, TQ, w), lambda bi, gi, i: (gi, 0, 0, 0)),
    ]
    args = [q_t, k, v_t, bias_t]
    scratch = [pltpu.VMEM((nq + NSA_TILES_PER_ITER if selected else WINDOW // TQ + 1, TQ, w), F32)]
    if selected:
        in_specs.append(pl.BlockSpec((1, 1, 1, LANES, TQ), lambda bi, gi, i: (bi, gi, i, 0, 0)))
        args.append(sel)
        scratch.append(pltpu.VMEM((LANES, TQ), F32))
    return pl.pallas_call(
        functools.partial(_nsa_attn_body, selected=selected, n_bias=n_bias),
        grid=(b, g, nq),
        in_specs=in_specs,
        out_specs=_o_spec(),
        out_shape=jax.ShapeDtypeStruct((b, t, NSA_Q), F32),
        scratch_shapes=scratch,
        compiler_params=_cparams(("parallel", "parallel", "parallel")),
        name="sel_attention" if selected else "win_attention",
    )(*args)


def _sb_body(q_ref, k_ref, v_ref, uo_ref, o_ref, c_ref):
    i = pl.program_id(2)
    uo = uo_ref[...]
    hb = c_ref.shape[0]
    pairs = range(hb // 2)
    row = lax.broadcasted_iota(jnp.int32, (TQ, TQ), 0)
    col = lax.broadcasted_iota(jnp.int32, (TQ, TQ), 1)
    rc = row - col
    first = col < HEAD_DIM
    c_ref[...] = jnp.zeros_like(c_ref)
    o_ref[...] = jnp.zeros_like(o_ref)
    qs = []
    for p in pairs:
        qp = q_ref[0, :, p * LANES:(p + 1) * LANES]
        zero = jnp.zeros_like(qp)
        qs += [jnp.where(first, qp, zero), jnp.where(first, zero, qp)]

    def cond(st):
        kt, cmax = st
        return (kt >= 0) & (cmax > SB_SKIP_LOG)

    def body(st):
        kt, _ = st
        start = pl.multiple_of(kt * TQ, TQ)
        mask = ((i - kt) * TQ + rc) > 0
        heads = range(hb)
        ks = [k_ref[0, pl.ds(start, TQ), p * LANES:(p + 1) * LANES] for p in pairs]
        vs = [v_ref[0, pl.ds(start, TQ), p * LANES:(p + 1) * LANES] for p in pairs]
        zs = [_dot_nt(qs[h], ks[h // 2]) for h in heads]
        lbs = [jnp.minimum(z, 0.0) - jnp.log(1.0 + jnp.exp(-jnp.abs(z))) for z in zs]
        parts = [_split_bf16(jnp.where(mask, lbs[h] - zs[h], 0.0)) for h in heads]
        sfxs = [_dot(hi, uo) + _dot(lo, uo) for hi, lo in parts]
        ws = [jnp.where(mask, jnp.exp(lbs[h] + sfxs[h][:, :TQ] + c_ref[h]), 0.0).astype(BF16) for h in heads]
        cm = None
        for p in pairs:
            o_ref[0, :, p * LANES:(p + 1) * LANES] += jnp.where(first, _dot(ws[2 * p], vs[p]),
                                                                _dot(ws[2 * p + 1], vs[p]))
        for h in heads:
            c = c_ref[h] + sfxs[h][:, TQ:]
            c_ref[h] = c
            cm = c if cm is None else jnp.maximum(cm, c)
        return kt - 1, jnp.max(cm)

    lax.while_loop(cond, body, (i, jnp.float32(0.0)))


def _stick_breaking(proj, col_q, col_k, col_v):
    b, t, _ = proj.shape
    hb = SB_HEADS_PER_STEP
    wb = hb * HEAD_DIM
    tri = np.triu(np.ones((TQ, TQ), np.float32), 0).T - np.eye(TQ, dtype=np.float32)
    uo = jnp.asarray(np.concatenate([tri, np.ones((TQ, TQ), np.float32)], axis=1), BF16)
    return pl.pallas_call(
        _sb_body,
        grid=(b, SB_HEADS // hb, t // TQ),
        in_specs=[
            pl.BlockSpec((1, TQ, wb), lambda bi, hi, i: (bi, i, col_q // wb + hi)),
            pl.BlockSpec((1, t, wb), lambda bi, hi, i: (bi, 0, col_k // wb + hi)),
            pl.BlockSpec((1, t, wb), lambda bi, hi, i: (bi, 0, col_v // wb + hi)),
            pl.BlockSpec((TQ, 2 * TQ), lambda bi, hi, i: (0, 0)),
        ],
        out_specs=pl.BlockSpec((1, TQ, wb), lambda bi, hi, i: (bi, i, hi)),
        out_shape=jax.ShapeDtypeStruct((b, t, SB_W), F32),
        scratch_shapes=[pltpu.VMEM((hb, TQ, TQ), F32)],
        compiler_params=_cparams(("parallel", "parallel", "parallel")),
        name="stick_breaking",
    )(proj, proj, proj, uo)


def _merge_body(oc_ref, os_ref, ow_ref, osb_ref, gbr_ref, ga_ref, gb_ref, e_ref, wn_ref, wsb_ref, o_ref):
    gate = jax.nn.sigmoid(gbr_ref[...].astype(F32))
    hi, lo = _split_bf16(gate)
    e = e_ref[...]
    gexp = _dot(hi, e) + _dot(lo, e)
    o_nsa = (gexp[:, :NSA_Q] * oc_ref[...] + gexp[:, NSA_Q:2 * NSA_Q] * os_ref[...]
             + gexp[:, 2 * NSA_Q:] * ow_ref[...])
    a = _dot(o_nsa.astype(BF16), wn_ref[...])
    bm = _dot(osb_ref[...].astype(BF16), wsb_ref[...])
    merged = jax.nn.sigmoid(ga_ref[...].astype(F32)) * a + jax.nn.sigmoid(gb_ref[...].astype(F32)) * bm
    o_ref[...] = merged.astype(o_ref.dtype)


def _merge(o_c, o_s, o_w, o_sb, proj, col_gbr, col_ga, col_gb, w_nsa, w_sb, tm=256):
    n = o_c.shape[0]
    dm = w_nsa.shape[1]
    e = np.zeros((LANES, 3 * NSA_Q), np.float32)
    for j in range(3 * NSA_HEADS):
        e[j, j * HEAD_DIM:(j + 1) * HEAD_DIM] = 1.0
    row = lambda i: (i, 0)
    full = lambda i: (0, 0)
    return pl.pallas_call(
        _merge_body,
        grid=(n // tm,),
        in_specs=[
            pl.BlockSpec((tm, NSA_Q), row), pl.BlockSpec((tm, NSA_Q), row), pl.BlockSpec((tm, NSA_Q), row),
            pl.BlockSpec((tm, SB_W), row),
            pl.BlockSpec((tm, LANES), lambda i: (i, col_gbr // LANES)),
            pl.BlockSpec((tm, dm), lambda i: (i, col_ga // dm)),
            pl.BlockSpec((tm, dm), lambda i: (i, col_gb // dm)),
            pl.BlockSpec((LANES, 3 * NSA_Q), full),
            pl.BlockSpec((NSA_Q, dm), full), pl.BlockSpec((SB_W, dm), full),
        ],
        out_specs=pl.BlockSpec((tm, dm), row),
        out_shape=jax.ShapeDtypeStruct((n, dm), BF16),
        compiler_params=_cparams(("parallel",)),
        name="gated_merge",
    )(o_c, o_s, o_w, o_sb, proj, proj, proj, jnp.asarray(e, BF16), w_nsa.astype(BF16), w_sb.astype(BF16))


def _peer_scores_body(wqt_ref, xt_ref, keys_ref, s_ref):
    qt = _dot(wqt_ref[...], xt_ref[...])
    c = keys_ref.shape[-1]
    for ch in range(keys_ref.shape[0]):
        qh, ql = _split_bf16(qt[ch * c:(ch + 1) * c])
        kh, kl = _split_bf16(keys_ref[ch])
        s_ref[ch * PEER_KEYS:(ch + 1) * PEER_KEYS, :] = _dot(kh, qh) + _dot(kh, ql) + _dot(kl, qh)


def _peer_scores(wq_t, x_t, keys, tn=512):
    rq, d = wq_t.shape
    n = x_t.shape[1]
    nch, nk, c = keys.shape
    return pl.pallas_call(
        _peer_scores_body,
        grid=(n // tn,),
        in_specs=[pl.BlockSpec((rq, d), lambda i: (0, 0)), pl.BlockSpec((d, tn), lambda i: (0, i)),
                  pl.BlockSpec((nch, nk, c), lambda i: (0, 0, 0))],
        out_specs=pl.BlockSpec((nch * nk, tn), lambda i: (0, i)),
        out_shape=jax.ShapeDtypeStruct((nch * nk, n), F32),
        compiler_params=_cparams(("parallel",)),
        name="peer_scores",
    )(wq_t, x_t, keys)


def _peer_route_body(s_ref, lim0_ref, c0_ref, rank1_ref, e1_ref):
    k = PEER_TOPK
    tn = s_ref.shape[-1]
    kidx = lax.broadcasted_iota(jnp.int32, (k, tn), 0)
    lg = int(math.log2(k))
    n_a, n_b = PEER_CAND_ROWS * k, PEER_CAND_COLS * k
    row = lax.broadcasted_iota(jnp.int32, (n_a + n_b, tn), 0)
    in_rows = row < n_a
    r0 = jnp.where(in_rows, jnp.right_shift(row, lg), jnp.bitwise_and(row - n_a, k - 1))
    r1 = jnp.where(in_rows, jnp.bitwise_and(row, k - 1), jnp.right_shift(row - n_a, lg))
    reachable = ((r0 + 1) * (r1 + 1) <= k) & (in_rows | (r0 >= PEER_CAND_ROWS))
    flat = (r0 * k + r1).astype(F32)
    for h in range(PEER_HEADS):
        s0 = s_ref[(2 * h) * PEER_KEYS:(2 * h + 1) * PEER_KEYS, :]
        s1 = s_ref[(2 * h + 1) * PEER_KEYS:(2 * h + 2) * PEER_KEYS, :]
        a, rank0 = _top_rows(s0, k)
        b, rank1 = _top_rows(s1, k)
        cand = jnp.concatenate([a[r:r + 1] + b for r in range(PEER_CAND_ROWS)]
                               + [a + b[c:c + 1] for c in range(PEER_CAND_COLS)], axis=0)
        cand = jnp.where(reachable, cand, -jnp.inf)
        best, crank = _top_rows(cand, k, order=flat)
        chosen = jnp.where(crank < float(k), 1.0, 0.0)
        e = chosen * jnp.exp(jnp.minimum(cand - best[0:1], 0.0))
        z = jnp.sum(e, axis=0, keepdims=True)
        cnt = chosen[n_a:n_a + k]
        for c in range(1, PEER_CAND_COLS):
            cnt = cnt + chosen[n_a + c * k:n_a + (c + 1) * k]
        for r in range(PEER_CAND_ROWS):
            n_r = jnp.sum(chosen[r * k:(r + 1) * k], axis=0, keepdims=True)
            cnt = jnp.where(kidx == r, n_r, cnt)
        lim0 = jnp.zeros_like(s0)
        for r in range(k):
            lim0 = jnp.where(rank0 == float(r), cnt[r:r + 1], lim0)
        lim0_ref[h] = lim0
        c0_ref[h] = jnp.exp(s0 - a[0:1]) / z
        rank1_ref[h] = rank1.astype(rank1_ref.dtype)
        e1_ref[h] = jnp.exp(s1 - b[0:1]).astype(e1_ref.dtype)


def _peer_route(s_t, tn=256):
    rows, n = s_t.shape
    shp = jax.ShapeDtypeStruct((PEER_HEADS, PEER_KEYS, n), F32)
    spec = pl.BlockSpec((PEER_HEADS, PEER_KEYS, tn), lambda i: (0, 0, i))
    return pl.pallas_call(
        _peer_route_body,
        grid=(n // tn,),
        in_specs=[pl.BlockSpec((rows, tn), lambda i: (0, i))],
        out_specs=[spec, spec, spec, spec],
        out_shape=[shp, shp, shp, shp],
        compiler_params=_cparams(("parallel",)),
        name="peer_route",
    )(s_t)


def _peer_main_body(u_ref, vt_ref, xt_ref, lim0_ref, c0_ref, rank1_in, e1_in, o_ref, act_ref, w_ref, rank1_ref,
                    e1_ref, *, te, ts):
    et = pl.program_id(1)

    @pl.when(et == 0)
    def _():
        o_ref[...] = jnp.zeros_like(o_ref)
        for h in range(PEER_HEADS):
            rank1_ref[h] = rank1_in[h].astype(BF16)
            e1_ref[h] = e1_in[h].astype(BF16)

    xt = xt_ref[...]
    tn = xt.shape[1]
    n_i = ts // PEER_KEYS
    for sb in range(te // ts):
        act_ref[sb] = _dot(u_ref[sb * ts:(sb + 1) * ts, :], xt)
    pk = 2 * SUBLANES
    grp = (PEER_KEYS // pk, pk, LANES)
    for sb in range(te // ts):
        for lc in range(tn // LANES):
            cols = slice(lc * LANES, (lc + 1) * LANES)
            ss = [jnp.zeros(grp, BF16) for _ in range(n_i)]
            for h in range(PEER_HEADS):
                r1 = rank1_ref[h, :, cols].reshape(grp)
                e1 = e1_ref[h, :, cols].reshape(grp)
                for ii in range(n_i):
                    r = sb * n_i + ii
                    lim = jnp.broadcast_to(lim0_ref[h, r:r + 1, cols], (pk, LANES)).astype(BF16)
                    cc = jnp.broadcast_to(c0_ref[h, r:r + 1, cols], (pk, LANES)).astype(BF16)
                    ss[ii] = ss[ii] + jnp.where(r1 < lim[None], e1 * cc[None], jnp.zeros((), BF16))
            for ii in range(n_i):
                rows = slice(ii * PEER_KEYS, (ii + 1) * PEER_KEYS)
                g = jax.nn.gelu(act_ref[sb, rows, cols]).astype(BF16)
                w_ref[sb, rows, cols] = ss[ii].reshape(PEER_KEYS, LANES) * g
        o_ref[...] += _dot(vt_ref[0, :, sb * ts:(sb + 1) * ts], w_ref[sb])


def _peer_main(u, v_t, x_t, lim0, c0, rank1, e1, tn=512, te=1024, ts=512):
    n_exp, d = u.shape
    n = x_t.shape[1]
    tab = pl.BlockSpec((PEER_HEADS, PEER_KEYS, tn), lambda i, j: (0, 0, i))
    tab0 = pl.BlockSpec((PEER_HEADS, te // PEER_KEYS, tn), lambda i, j: (0, j, i))
    return pl.pallas_call(
        functools.partial(_peer_main_body, te=te, ts=ts),
        grid=(n // tn, n_exp // te),
        in_specs=[pl.BlockSpec((te, d), lambda i, j: (j, 0)), pl.BlockSpec((1, d, te), lambda i, j: (j, 0, 0)),
                  pl.BlockSpec((d, tn), lambda i, j: (0, i)), tab0, tab0, tab, tab],
        out_specs=pl.BlockSpec((d, tn), lambda i, j: (0, i)),
        out_shape=jax.ShapeDtypeStruct((d, n), F32),
        scratch_shapes=[pltpu.VMEM((te // ts, ts, tn), F32), pltpu.VMEM((te // ts, ts, tn), BF16),
                        pltpu.VMEM((PEER_HEADS, PEER_KEYS, tn), BF16), pltpu.VMEM((PEER_HEADS, PEER_KEYS, tn), BF16)],
        compiler_params=_cparams(("parallel", "arbitrary"), vmem_mb=56),
        name="peer_experts",
    )(u, v_t, x_t, lim0, c0, rank1, e1)


def kernel(x, attn_norm_g, w_in, cmp_k_pe, cmp_k_w1, cmp_k_w2, cmp_v_pe, cmp_v_w1, cmp_v_w2, rel_bias_table,
           w_branch_nsa, w_branch_sb, w_out, ffn_norm_g, peer_w_q, peer_sub_keys, peer_u, peer_v, final_norm_g):
    h = x
    for l in range(attn_norm_g.shape[0]):
        h = _layer(h, attn_norm_g[l], w_in[l], cmp_k_pe[l], cmp_k_w1[l], cmp_k_w2[l], cmp_v_pe[l], cmp_v_w1[l],
                   cmp_v_w2[l], rel_bias_table, w_branch_nsa[l], w_branch_sb[l], w_out[l], ffn_norm_g[l],
                   peer_w_q[l], peer_sub_keys[l], peer_u[l], peer_v[l],
                   final_norm_g if l == attn_norm_g.shape[0] - 1 else None)
    return h


def _layer(h, attn_g, w_in, ck_pe, ck_w1, ck_w2, cv_pe, cv_w1, cv_w2, rel_table, w_br_nsa, w_br_sb, w_out,
           ffn_g, pq, psk, pu, pv, final_g):
    b, t, dm = h.shape
    n = b * t
    g, hg, d = NSA_GROUPS, NSA_HPG, HEAD_DIM
    nq = t // TQ
    x2 = h.reshape(n, dm)

    o_gbr = NSA_Q + 6 * NSA_KV
    n_gbr = 3 * NSA_HEADS
    o_qb = o_gbr + n_gbr
    o_ga = o_qb + 3 * SB_W
    tn_in = 768
    packed = w_in.shape[1]
    packed_pad = -(-packed // tn_in) * tn_in
    scale = HEAD_DIM ** -0.5
    w_pack = jnp.concatenate([w_in[:, o_ga:], w_in[:, o_qb:o_qb + SB_W] * scale, w_in[:, o_qb + SB_W:o_ga],
                              w_in[:, :NSA_Q] * scale, w_in[:, NSA_Q:o_gbr], w_in[:, o_gbr:o_qb],
                              jnp.zeros((dm, packed_pad - packed), w_in.dtype)], axis=1).astype(BF16)
    a = _rmsnorm(x2, attn_g, BF16)
    proj = _matmul(a, w_pack, 2048, tn_in, BF16, name="in_proj")
    proj3 = proj.reshape(b, t, packed_pad)
    col_ga, col_gb = 0, dm
    col_qb = 2 * dm
    col_qn = col_qb + 3 * SB_W
    col_kv = col_qn + NSA_Q
    col_gbr = col_kv + 6 * NSA_KV
    assert col_qb % (SB_HEADS_PER_STEP * d) == 0 and SB_W % (SB_HEADS_PER_STEP * d) == 0
    assert col_qn % (hg * d) == 0 and col_gbr % LANES == 0 and dm % LANES == 0

    def heads_kv(z):
        return z.reshape(b, t, g, d).transpose(0, 2, 1, 3)

    def tiles_kv_t(z):
        return z.reshape(b, nq, TQ, g, d).transpose(0, 3, 1, 4, 2)

    kv_cols = [proj[:, col_kv + j * NSA_KV:col_kv + (j + 1) * NSA_KV] for j in range(6)]
    kc_tok, vc_tok = heads_kv(kv_cols[0]), heads_kv(kv_cols[1])
    ks, vs_t = heads_kv(kv_cols[2]), tiles_kv_t(kv_cols[3])
    kw, vw_t = heads_kv(kv_cols[4]), tiles_kv_t(kv_cols[5])

    n_cmp = (t - CMP_LEN) // CMP_STRIDE + 1
    n_chunk = t // CMP_STRIDE
    ncp = -(-n_cmp // LANES) * LANES
    reps = CMP_LEN // CMP_STRIDE

    def blocks(tok):
        ch = tok.reshape(b, g, n_chunk, CMP_STRIDE * d)
        ch = jnp.pad(ch, ((0, 0), (0, 0), (0, ncp + reps - 1 - n_chunk), (0, 0)))
        blk = jnp.concatenate([ch[:, :, r:r + ncp] for r in range(reps)], axis=-1)
        return blk.reshape(b * g * ncp, CMP_LEN * d)

    kc_blk = _compress(blocks(kc_tok), ck_pe.reshape(1, CMP_LEN * d), ck_w1, ck_w2, ncp).reshape(b, g, ncp, d)
    vc_blk = _compress(blocks(vc_tok), cv_pe.reshape(1, CMP_LEN * d), cv_w1, cv_w2, ncp).reshape(b, g, ncp, d)

    r_i = jnp.arange(TQ, dtype=jnp.int32)
    per_tile = TQ // CMP_STRIDE
    rows_c = ncp + per_tile * (nq - 1)
    shift_c = per_tile * (nq - 1) - jnp.arange(rows_c, dtype=jnp.int32)
    dist_c = CMP_STRIDE * shift_c[None, :, None] + r_i[None, None, :] - (CMP_LEN - 1)
    bias_c = _bias_expand(rel_table, _rel_bucket(dist_c))
    n_bias = min(nq, -(-(REL_MAX_DIST + TQ - 1) // TQ) + 1)
    dist_t = jnp.arange(n_bias, dtype=jnp.int32)[:, None, None] * TQ + r_i[None, None, :] - r_i[None, :, None]
    bias_t = _bias_expand(rel_table, _rel_bucket(dist_t))

    n_sel = t // SEL_LEN
    assert n_sel <= LANES and t % TQ == 0 and t >= WINDOW + TQ, "selection table holds one block per row of a tile"
    nsp = min(LANES, -(-n_sel // 16) * 16)
    c_start = np.arange(ncp) * CMP_STRIDE
    s_start = np.arange(nsp) * SEL_LEN
    overlap_t = np.maximum(np.minimum(c_start[None, :] + CMP_LEN, s_start[:, None] + SEL_LEN)
                           - np.maximum(c_start[None, :], s_start[:, None]), 0).astype(np.float32) / CMP_LEN
    overlap_t[:, n_cmp:] = 0.0
    overlap_t[n_sel:, :] = 0.0
    o_c, sel, q_t = _cmp_attention(proj3, col_qn, kc_blk, vc_blk.transpose(0, 1, 3, 2), bias_c,
                                   jnp.asarray(overlap_t, BF16), n_cmp, min(SEL_TOPK, n_sel))
    o_s = _nsa_attention(q_t, ks, vs_t, bias_t, sel)
    o_w = _nsa_attention(q_t, kw, vw_t, bias_t)
    o_sb = _stick_breaking(proj3, col_qb, col_qb + SB_W, col_qb + 2 * SB_W)

    merged = _merge(o_c.reshape(n, NSA_Q), o_s.reshape(n, NSA_Q), o_w.reshape(n, NSA_Q), o_sb.reshape(n, SB_W),
                    proj, col_gbr, col_ga, col_gb, w_br_nsa, w_br_sb)
    h1 = _matmul(merged, w_out.astype(BF16), 1024, 1024, F32, res=x2, name="out_proj")

    xn_t = _rmsnorm_t(h1, ffn_g, BF16)
    keys = psk.reshape(PEER_HEADS * 2, PEER_KEYS, -1)
    s_t = _peer_scores(pq.T.astype(BF16), xn_t, keys)
    lim0, c0, rank1, e1 = _peer_route(s_t)
    te = 1024
    pv_t = pv.reshape(-1, te, dm).transpose(0, 2, 1).astype(BF16)
    ffn_t = _peer_main(pu.astype(BF16), pv_t, xn_t, lim0, c0, rank1, e1, te=te)
    if final_g is None:
        return (h1 + ffn_t.T).reshape(b, t, dm)
    return _add_rmsnorm(h1, ffn_t, final_g).reshape(b, t, dm)
```
